```python
import math
import jax, jax.numpy as jnp
from jax import lax
import numpy as np

D_MODEL = 1024
BATCH = 16
SEQ = 2048
DEPTH = 2
DEC_BATCH = 32
DEC_SEQ = 4
PAST_LEN = 16384
PAGE_SIZE = 128

HEAD_DIM = 64
H_FOX = 8
H_DIFF = 4
H_SB = D_MODEL // HEAD_DIM
FOX_W = H_FOX * HEAD_DIM
DIFF_W = H_DIFF * 2 * HEAD_DIM
MIX_W = FOX_W + DIFF_W
AB_SPLITS = [FOX_W, 2 * FOX_W, 3 * FOX_W, 3 * FOX_W + H_FOX,
             3 * FOX_W + H_FOX + DIFF_W, 3 * FOX_W + H_FOX + 2 * DIFF_W]
D_IN_AB = 3 * FOX_W + H_FOX + 3 * DIFF_W
D_IN_SB = 3 * H_SB * HEAD_DIM
D_FF = 2816
N_BUCKETS = 32
MAX_EXACT = N_BUCKETS // 2
MAX_DISTANCE = 128
Q_BLOCK = 128
RMS_EPS = 1e-5
NEG = -1e30
N_AB_LAYERS = (DEPTH + 1) // 2
N_C_LAYERS = DEPTH // 2

kernel_name = 'fox_diff_stickbreak_macaron_decode_step'


def _rmsnorm(x, g):
    xf = x.astype(jnp.float32)
    y = xf * lax.rsqrt(jnp.mean(xf * xf, axis=-1, keepdims=True) + RMS_EPS)
    return (y * g.astype(jnp.float32)).astype(x.dtype)


def _half_ffn(x, g, w_up, w_down):
    gate, up = jnp.split(_rmsnorm(x, g) @ w_up, 2, axis=-1)
    return 0.5 * ((jax.nn.silu(gate) * up) @ w_down)


def _rev_cumsum_excl(a, axis):
    axis = axis % a.ndim
    return lax.cumsum(a, axis=axis, reverse=True) - a


def _t5_bias(table, q_pos, k_pos):
    n = jnp.maximum(q_pos[:, None] - k_pos[None, :], 0)
    nf = jnp.maximum(n, MAX_EXACT).astype(jnp.float32)
    large = MAX_EXACT + (jnp.log(nf / MAX_EXACT) / math.log(MAX_DISTANCE / MAX_EXACT)
                         * (N_BUCKETS - MAX_EXACT)).astype(jnp.int32)
    bucket = jnp.where(n < MAX_EXACT, n, jnp.minimum(large, N_BUCKETS - 1))
    return jnp.moveaxis(table[bucket], -1, 0).astype(jnp.float32)


def _sweep(fn, qs, q_pos):
    t = q_pos.shape[0]
    nb = t // Q_BLOCK
    split = lambda a: jnp.swapaxes(a.reshape(a.shape[0], nb, Q_BLOCK, *a.shape[2:]), 0, 1)
    xs = tuple(split(a) for a in qs) + (q_pos.reshape(nb, Q_BLOCK),)
    out = jnp.swapaxes(lax.map(lambda args: fn(*args), xs), 0, 1)
    return out.reshape(out.shape[0], t, *out.shape[3:])


def _fox_core(q, k, v, fq, fk, q_pos, k_pos):
    s = jnp.einsum('bqhd,bkhd->bhqk', q, k).astype(jnp.float32) * HEAD_DIM ** -0.5
    s = s + jnp.swapaxes(fq, 1, 2)[..., :, None] - fk[:, :, None, :]
    mask = k_pos[None, :] <= q_pos[:, None]
    p = jax.nn.softmax(jnp.where(mask, s, NEG), axis=-1)
    return jnp.einsum('bhqk,bkhd->bqhd', p.astype(v.dtype), v)


def _diff_core(q, k, v, table, lam, q_pos, k_pos):
    s = jnp.einsum('bqhcd,bkhcd->bhcqk', q, k).astype(jnp.float32) * HEAD_DIM ** -0.5
    s = s + _t5_bias(table, q_pos, k_pos)[None, :, None]
    mask = k_pos[None, :] <= q_pos[:, None]
    p = jax.nn.softmax(jnp.where(mask, s, NEG), axis=-1)
    a = p[:, :, 0] - lam * p[:, :, 1]
    return jnp.einsum('bhqk,bkhe->bqhe', a.astype(v.dtype), v)


def _sb_core(q, k, v, q_pos, k_pos):
    z = jnp.einsum('bqhd,bkhd->bhqk', q, k).astype(jnp.float32) * HEAD_DIM ** -0.5
    strict = k_pos[None, :] < q_pos[:, None]
    log_rest = jnp.where(strict, jax.nn.log_sigmoid(-z), 0.0)
    log_a = jax.nn.log_sigmoid(z) + _rev_cumsum_excl(log_rest, -1)
    a = jnp.where(strict, jnp.exp(log_a), 0.0)
    return jnp.einsum('bhqk,bkhd->bqhd', a.astype(v.dtype), v)


def _diff_lambda(lam_p, lam_init):
    lp = lam_p.astype(jnp.float32)
    return jnp.exp(jnp.sum(lp[0] * lp[1])) - jnp.exp(jnp.sum(lp[2] * lp[3])) + lam_init


def _gather_pages(pool, layer, page_table):
    pages = pool[layer, page_table]
    return pages.reshape(pages.shape[0], -1, *pages.shape[3:])


def _ab_mixer(h, past, w_in, b_f, lam_p, subln_g, w_out, rel_table, lam_init):
    b, t, _ = h.shape
    q_a, k_a, v_a, f_a, q_b, k_b, v_b = jnp.split(h @ w_in, AB_SPLITS, axis=-1)
    q_a = q_a.reshape(b, t, H_FOX, HEAD_DIM)
    k_a = k_a.reshape(b, t, H_FOX, HEAD_DIM)
    v_a = v_a.reshape(b, t, H_FOX, HEAD_DIM)
    q_b = q_b.reshape(b, t, H_DIFF, 2, HEAD_DIM)
    k_b = k_b.reshape(b, t, H_DIFF, 2, HEAD_DIM)
    v_b = v_b.reshape(b, t, H_DIFF, 2 * HEAD_DIM)
    logf = jax.nn.log_sigmoid((f_a + b_f).astype(jnp.float32))
    f_new = jnp.cumsum(logf, axis=1)
    if past is None:
        p_len = 0
        ka_all, va_all, f_all, kb_all, vb_all = k_a, v_a, f_new, k_b, v_b
    else:
        kv_a_past, logf_past, kv_b_past = past
        p_len = kv_a_past.shape[1]
        ka_all = jnp.concatenate([kv_a_past[:, :, 0], k_a], axis=1)
        va_all = jnp.concatenate([kv_a_past[:, :, 1], v_a], axis=1)
        f_past = -_rev_cumsum_excl(logf_past.astype(jnp.float32), 1)
        f_all = jnp.concatenate([f_past, f_new], axis=1)
        kb_past = kv_b_past[:, :, 0].reshape(b, p_len, H_DIFF, 2, HEAD_DIM)
        kb_all = jnp.concatenate([kb_past, k_b], axis=1)
        vb_all = jnp.concatenate([kv_b_past[:, :, 1], v_b], axis=1)
    q_pos = p_len + jnp.arange(t, dtype=jnp.int32)
    k_pos = jnp.arange(p_len + t, dtype=jnp.int32)
    fk = jnp.swapaxes(f_all, 1, 2)
    lam = _diff_lambda(lam_p, lam_init)
    fox = lambda q, fq, qp: _fox_core(q, ka_all, va_all, fq, fk, qp, k_pos)
    diff = lambda q, qp: _diff_core(q, kb_all, vb_all, rel_table, lam, qp, k_pos)
    if past is None:
        o_a = _sweep(fox, (q_a, f_new), q_pos)
        o_b = _sweep(diff, (q_b,), q_pos)
    else:
        o_a = fox(q_a, f_new, q_pos)
        o_b = diff(q_b, q_pos)
    o_b = _rmsnorm(o_b, subln_g) * (1.0 - lam_init)
    o = jnp.concatenate([o_a.reshape(b, t, FOX_W), o_b.reshape(b, t, DIFF_W)], axis=-1)
    kv_a = jnp.stack([k_a, v_a], axis=2)
    kv_b = jnp.stack([k_b.reshape(b, t, H_DIFF, 2 * HEAD_DIM), v_b], axis=2)
    return o @ w_out, kv_a, logf, kv_b


def _sb_mixer(h, past, w_in, w_out):
    b, t, _ = h.shape
    q, k, v = jnp.split(h @ w_in, 3, axis=-1)
    q = q.reshape(b, t, H_SB, HEAD_DIM)
    k = k.reshape(b, t, H_SB, HEAD_DIM)
    v = v.reshape(b, t, H_SB, HEAD_DIM)
    if past is None:
        p_len = 0
        k_all, v_all = k, v
    else:
        p_len = past.shape[1]
        k_all = jnp.concatenate([past[:, :, 0], k], axis=1)
        v_all = jnp.concatenate([past[:, :, 1], v], axis=1)
    q_pos = p_len + jnp.arange(t, dtype=jnp.int32)
    k_pos = jnp.arange(p_len + t, dtype=jnp.int32)
    core = lambda qb, qp: _sb_core(qb, k_all, v_all, qp, k_pos)
    o = _sweep(core, (q,), q_pos) if past is None else core(q, q_pos)
    return o.reshape(b, t, H_SB * HEAD_DIM) @ w_out, jnp.stack([k, v], axis=2)


def setup_inputs(seed: int = 0) -> dict:
    key = jax.random.key(seed)
    ks = jax.random.split(key, 20)
    n_pages = PAST_LEN // PAGE_SIZE
    n_used = DEC_BATCH * n_pages
    n_pool = n_used + n_used // 4
    page_table = jax.random.permutation(ks[0], n_pool)[:n_used].reshape(DEC_BATCH, n_pages).astype(jnp.int32)
    nrm = lambda k, shape, scale=1.0: scale * jax.random.normal(k, shape, jnp.float32)
    return {
        'x_prompt': nrm(ks[1], (BATCH, SEQ, D_MODEL)),
        'x_sample': nrm(ks[2], (DEC_BATCH, DEC_SEQ, D_MODEL)),
        'cache_fox_kv': nrm(ks[3], (N_AB_LAYERS, n_pool, PAGE_SIZE, 2, H_FOX, HEAD_DIM)),
        'cache_fox_logf': jax.nn.log_sigmoid(nrm(ks[4], (N_AB_LAYERS, n_pool, PAGE_SIZE, H_FOX))),
        'cache_diff_kv': nrm(ks[5], (N_AB_LAYERS, n_pool, PAGE_SIZE, 2, H_DIFF, 2 * HEAD_DIM)),
        'cache_sb_kv': nrm(ks[6], (N_C_LAYERS, n_pool, PAGE_SIZE, 2, H_SB, HEAD_DIM)),
        'page_table': page_table,
        'rel_bias_table': nrm(ks[7], (N_BUCKETS, H_DIFF), 0.5),
        'norm_g': 1.0 + nrm(ks[8], (DEPTH, 3, D_MODEL), 0.1),
        'final_norm_g': 1.0 + nrm(ks[9], (D_MODEL,), 0.1),
        'w_ffn_up': nrm(ks[10], (DEPTH, 2, D_MODEL, 2 * D_FF), D_MODEL ** -0.5),
        'w_ffn_down': nrm(ks[11], (DEPTH, 2, D_FF, D_MODEL), D_FF ** -0.5),
        'w_in_ab': nrm(ks[12], (N_AB_LAYERS, D_MODEL, D_IN_AB), D_MODEL ** -0.5),
        'b_forget': nrm(ks[13], (N_AB_LAYERS, H_FOX), 0.1),
        'diff_lambda': nrm(ks[14], (N_AB_LAYERS, 4, HEAD_DIM), 0.1),
        'diff_subln_g': 1.0 + nrm(ks[15], (N_AB_LAYERS, 2 * HEAD_DIM), 0.1),
        'w_out_ab': nrm(ks[16], (N_AB_LAYERS, MIX_W, D_MODEL), MIX_W ** -0.5),
        'w_in_sb': nrm(ks[17], (N_C_LAYERS, D_MODEL, D_IN_SB), D_MODEL ** -0.5),
        'w_out_sb': nrm(ks[18], (N_C_LAYERS, H_SB * HEAD_DIM, D_MODEL), (H_SB * HEAD_DIM) ** -0.5),
    }


def reference(x_prompt, x_sample, cache_fox_kv, cache_fox_logf, cache_diff_kv, cache_sb_kv,
              page_table, rel_bias_table, norm_g, final_norm_g, w_ffn_up, w_ffn_down,
              w_in_ab, b_forget, diff_lambda, diff_subln_g, w_out_ab, w_in_sb, w_out_sb):
    xp, xs = x_prompt, x_sample
    p_fox_kv, p_fox_lf, p_diff_kv, p_sb_kv = [], [], [], []
    s_fox_kv, s_fox_lf, s_diff_kv, s_sb_kv = [], [], [], []
    for i in range(DEPTH):
        l = i // 2
        xp = xp + _half_ffn(xp, norm_g[i, 0], w_ffn_up[i, 0], w_ffn_down[i, 0])
        xs = xs + _half_ffn(xs, norm_g[i, 0], w_ffn_up[i, 0], w_ffn_down[i, 0])
        hp = _rmsnorm(xp, norm_g[i, 1])
        hs = _rmsnorm(xs, norm_g[i, 1])
        if i % 2 == 0:
            lam_init = 0.8 - 0.6 * math.exp(-0.3 * i)
            dp, kv_a, lf, kv_b = _ab_mixer(hp, None, w_in_ab[l], b_forget[l], diff_lambda[l],
                                           diff_subln_g[l], w_out_ab[l], rel_bias_table, lam_init)
            past = (_gather_pages(cache_fox_kv, l, page_table),
                    _gather_pages(cache_fox_logf, l, page_table),
                    _gather_pages(cache_diff_kv, l, page_table))
            ds, kv_a_s, lf_s, kv_b_s = _ab_mixer(hs, past, w_in_ab[l], b_forget[l], diff_lambda[l],
                                                 diff_subln_g[l], w_out_ab[l], rel_bias_table, lam_init)
            p_fox_kv.append(kv_a); p_fox_lf.append(lf); p_diff_kv.append(kv_b)
            s_fox_kv.append(kv_a_s); s_fox_lf.append(lf_s); s_diff_kv.append(kv_b_s)
        else:
            dp, kv_c = _sb_mixer(hp, None, w_in_sb[l], w_out_sb[l])
            ds, kv_c_s = _sb_mixer(hs, _gather_pages(cache_sb_kv, l, page_table), w_in_sb[l], w_out_sb[l])
            p_sb_kv.append(kv_c); s_sb_kv.append(kv_c_s)
        xp = xp + dp
        xs = xs + ds
        xp = xp + _half_ffn(xp, norm_g[i, 2], w_ffn_up[i, 1], w_ffn_down[i, 1])
        xs = xs + _half_ffn(xs, norm_g[i, 2], w_ffn_up[i, 1], w_ffn_down[i, 1])
    y_prompt = _rmsnorm(xp, final_norm_g)
    y_sample = _rmsnorm(xs, final_norm_g)
    return (y_prompt, y_sample,
            jnp.stack(p_fox_kv), jnp.stack(p_fox_lf), jnp.stack(p_diff_kv), jnp.stack(p_sb_kv),
            jnp.stack(s_fox_kv), jnp.stack(s_fox_lf), jnp.stack(s_diff_kv), jnp.stack(s_sb_kv))
```

```python
import functools
import math

import jax
import jax.numpy as jnp
from jax import lax
from jax.experimental import pallas as pl
from jax.experimental.pallas import tpu as pltpu

F32 = jnp.float32
BF16 = jnp.bfloat16

HEAD_DIM = 64
H_FOX = 8
H_DIFF = 4
FOX_W = H_FOX * HEAD_DIM
DIFF_W = H_DIFF * 2 * HEAD_DIM
N_BUCKETS = 32
MAX_EXACT = N_BUCKETS // 2
MAX_DISTANCE = 128
RMS_EPS = 1e-5
NEG = -1e30
Q_SCALE = HEAD_DIM ** -0.5

LANES = 128
F_PAD = LANES
VMEM_LIMIT = 48 * 1024 * 1024

_NT = (((1,), (1,)), ((), ()))


def _cparams(*sem):
    return pltpu.CompilerParams(dimension_semantics=sem, vmem_limit_bytes=VMEM_LIMIT)


def _rmsnorm(x, g):
    return x * lax.rsqrt(jnp.mean(x * x, axis=-1, keepdims=True) + RMS_EPS) * g


def _log_sigmoid(x):
    return jnp.minimum(x, 0.0) - jnp.log(1.0 + jnp.exp(-jnp.abs(x)))


def _pick_tile(n, target):
    t = min(n, target)
    while n % t:
        t //= 2
    return t


def _ffn_kernel(x_ref, g_ref, wg_ref, wu_ref, wd_ref, *rest, final):
    if final:
        fg_ref, o_ref, h_ref, acc_ref = rest
    else:
        o_ref, h_ref, acc_ref = rest
    j = pl.program_id(1)

    @pl.when(j == 0)
    def _():
        h_ref[...] = _rmsnorm(x_ref[...], g_ref[...]).astype(BF16)
        acc_ref[...] = jnp.zeros_like(acc_ref)

    h = h_ref[...]
    gate = jnp.dot(h, wg_ref[...], preferred_element_type=F32)
    up = jnp.dot(h, wu_ref[...], preferred_element_type=F32)
    act = (gate * jax.nn.sigmoid(gate) * up).astype(BF16)
    acc_ref[...] += jnp.dot(act, wd_ref[...], preferred_element_type=F32)

    @pl.when(j == pl.num_programs(1) - 1)
    def _():
        y = x_ref[...] + 0.5 * acc_ref[...]
        if final:
            y = _rmsnorm(y, fg_ref[...])
        o_ref[...] = y


def _ffn(x, g, w_up, w_down, final_g=None, tm_target=512, tf_target=256):
    m, d = x.shape
    d_ff = w_down.shape[0]
    tm = _pick_tile(m, tm_target)
    tf = _pick_tile(d_ff, tf_target)
    nf = d_ff // tf
    final = final_g is not None
    in_specs = [
        pl.BlockSpec((tm, d), lambda i, j: (i, 0)),
        pl.BlockSpec((1, d), lambda i, j: (0, 0)),
        pl.BlockSpec((d, tf), lambda i, j: (0, j)),
        pl.BlockSpec((d, tf), lambda i, j: (0, j + nf)),
        pl.BlockSpec((tf, d), lambda i, j: (j, 0)),
    ]
    args = [x, g.reshape(1, d), w_up, w_up, w_down]
    if final:
        in_specs.append(pl.BlockSpec((1, d), lambda i, j: (0, 0)))
        args.append(final_g.reshape(1, d))
    return pl.pallas_call(
        functools.partial(_ffn_kernel, final=final),
        grid=(m // tm, nf),
        in_specs=in_specs,
        out_specs=pl.BlockSpec((tm, d), lambda i, j: (i, 0)),
        out_shape=jax.ShapeDtypeStruct((m, d), F32),
        scratch_shapes=[pltpu.VMEM((tm, d), BF16), pltpu.VMEM((tm, d), F32)],
        compiler_params=_cparams("parallel", "arbitrary"),
        name="half_ffn",
    )(*args)


def _proj_kernel(x_ref, g_ref, wq_ref, *rest, n_kv, with_f):
    kv_w = rest[:n_kv]
    rest = rest[n_kv:]
    if with_f:
        wf_ref, bf_ref = rest[:2]
        rest = rest[2:]
    qkv_ref = rest[0]
    kv_o = rest[1:1 + n_kv]
    h = _rmsnorm(x_ref[...], g_ref[...]).astype(BF16)
    q = jnp.dot(h, wq_ref[...], preferred_element_type=F32) * Q_SCALE
    wq_cols = wq_ref.shape[1]
    qkv_ref[:, :wq_cols] = q.astype(BF16)
    off = wq_cols
    for w_ref, o_ref in zip(kv_w, kv_o):
        kv = jnp.dot(h, w_ref[...], preferred_element_type=F32)
        o_ref[...] = kv
        qkv_ref[:, off:off + kv.shape[1]] = kv.astype(BF16)
        off += kv.shape[1]
    if with_f:
        lf_ref = rest[1 + n_kv]
        f = jnp.dot(h, wf_ref[...], preferred_element_type=F32) + bf_ref[...]
        lf_ref[...] = _log_sigmoid(f)[:, :lf_ref.shape[1]]


def _proj(x, g, wq, kv_ws, wf=None, bf=None, n_f=0, tm_target=512):
    m, d = x.shape
    tm = _pick_tile(m, tm_target)
    with_f = wf is not None
    n_kv = len(kv_ws)
    tot = wq.shape[1] + sum(w.shape[1] for w in kv_ws)
    const = lambda i: (0, 0)
    row = lambda i: (i, 0)
    in_specs = [pl.BlockSpec((tm, d), row), pl.BlockSpec((1, d), const),
                pl.BlockSpec(wq.shape, const)]
    in_specs += [pl.BlockSpec(w.shape, const) for w in kv_ws]
    args = [x, g.reshape(1, d), wq, *kv_ws]
    out_shape = [jax.ShapeDtypeStruct((m, tot), BF16)]
    out_specs = [pl.BlockSpec((tm, tot), row)]
    for w in kv_ws:
        out_shape.append(jax.ShapeDtypeStruct((m, w.shape[1]), F32))
        out_specs.append(pl.BlockSpec((tm, w.shape[1]), row))
    if with_f:
        in_specs += [pl.BlockSpec(wf.shape, const), pl.BlockSpec(bf.shape, const)]
        args += [wf, bf]
        out_shape.append(jax.ShapeDtypeStruct((m, n_f), F32))
        out_specs.append(pl.BlockSpec((tm, n_f), row))
    return pl.pallas_call(
        functools.partial(_proj_kernel, n_kv=n_kv, with_f=with_f),
        grid=(m // tm,),
        in_specs=in_specs,
        out_specs=out_specs,
        out_shape=out_shape,
        compiler_params=_cparams("parallel"),
        name="in_proj",
    )(*args)


def _outproj_kernel(x_ref, *rest, n_parts):
    o_refs = rest[:n_parts]
    w_refs = rest[n_parts:2 * n_parts]
    y_ref = rest[2 * n_parts]
    acc = x_ref[...]
    for o_ref, w_ref in zip(o_refs, w_refs):
        acc = acc + jnp.dot(o_ref[...].astype(BF16), w_ref[...], preferred_element_type=F32)
    y_ref[...] = acc


def _outproj(x, parts, ws, tm_target=512):
    m, d = x.shape
    tm = _pick_tile(m, tm_target)
    row = lambda i: (i, 0)
    const = lambda i: (0, 0)
    in_specs = [pl.BlockSpec((tm, d), row)]
    in_specs += [pl.BlockSpec((tm, p.shape[1]), row) for p in parts]
    in_specs += [pl.BlockSpec(w.shape, const) for w in ws]
    return pl.pallas_call(
        functools.partial(_outproj_kernel, n_parts=len(parts)),
        grid=(m // tm,),
        in_specs=in_specs,
        out_specs=pl.BlockSpec((tm, d), row),
        out_shape=jax.ShapeDtypeStruct((m, d), F32),
        compiler_params=_cparams("parallel"),
        name="out_proj",
    )(x, *parts, *ws)


def _cumsum_lanes_kernel(x_ref, o_ref):
    x = x_ref[0]
    t = x.shape[1]
    lane = lax.broadcasted_iota(jnp.int32, x.shape, 1)
    d = 1
    while d < t:
        x = x + jnp.where(lane >= d, pltpu.roll(x, d, axis=1), 0.0)
        d *= 2
    o_ref[0] = x


def _cumsum_lanes(x):
    b, h, t = x.shape
    spec = pl.BlockSpec((1, h, t), lambda i: (i, 0, 0))
    return pl.pallas_call(
        _cumsum_lanes_kernel, grid=(b,), in_specs=[spec], out_specs=spec,
        out_shape=jax.ShapeDtypeStruct(x.shape, F32),
        compiler_params=_cparams("parallel"), name="logf_cumsum",
    )(x)


def _bucket(n):
    nf = jnp.maximum(n, MAX_EXACT).astype(F32)
    large = MAX_EXACT + (jnp.log(nf / MAX_EXACT) / math.log(MAX_DISTANCE / MAX_EXACT)
                         * (N_BUCKETS - MAX_EXACT)).astype(jnp.int32)
    return jnp.where(n < MAX_EXACT, n, jnp.minimum(large, N_BUCKETS - 1))


def _bias_kernel(tbl_ref, blk_ref, dec_ref, *, tq, n_q, page):
    h = pl.program_id(0)
    r = lax.broadcasted_iota(jnp.int32, (tq, tq), 0)
    c = lax.broadcasted_iota(jnp.int32, (tq, tq), 1)
    for which, base in enumerate((0, tq)):
        bucket = _bucket(jnp.maximum(r - c + base, 0))
        val = jnp.zeros((tq, tq), F32)
        for b in range(N_BUCKETS):
            val = jnp.where(bucket == b, tbl_ref[b, h], val)
        blk_ref[0, which] = val
    n_rows = n_q * 2 * H_DIFF
    rr = lax.broadcasted_iota(jnp.int32, (n_rows, page), 0)
    jj = lax.broadcasted_iota(jnp.int32, (n_rows, page), 1)
    qi = rr // (2 * H_DIFF)
    hd = (rr % (2 * H_DIFF)) // 2
    dists = (page + qi - jj, jnp.maximum(qi - jj, 0), jnp.full((n_rows, page), MAX_DISTANCE, jnp.int32))
    for which, dist in enumerate(dists):
        bucket = _bucket(dist)
        val = jnp.zeros((n_rows, page), F32)
        for b in range(N_BUCKETS):
            tb = jnp.zeros((n_rows, page), F32)
            for hh in range(H_DIFF):
                tb = jnp.where(hd == hh, tbl_ref[b, hh], tb)
            val = jnp.where(bucket == b, tb, val)
        dec_ref[which] = val


def _bias_tables(table, tq, n_q, page):
    n_rows = n_q * 2 * H_DIFF
    return pl.pallas_call(
        functools.partial(_bias_kernel, tq=tq, n_q=n_q, page=page),
        grid=(H_DIFF,),
        in_specs=[pl.BlockSpec(memory_space=pltpu.SMEM)],
        out_specs=[pl.BlockSpec((1, 2, tq, tq), lambda h: (h, 0, 0, 0)),
                   pl.BlockSpec((3, n_rows, page), lambda h: (0, 0, 0))],
        out_shape=[jax.ShapeDtypeStruct((H_DIFF, 2, tq, tq), F32),
                   jax.ShapeDtypeStruct((3, n_rows, page), F32)],
        compiler_params=_cparams("arbitrary"),
        name="rel_bias_tables",
    )(table)


def _half_masks(tq):
    lane = lax.broadcasted_iota(jnp.int32, (tq, LANES), 1)
    return lane < HEAD_DIM


def _online_softmax_step(s, v, m, l, acc):
    m_new = jnp.maximum(m, jnp.max(s, axis=-1, keepdims=True))
    alpha = jnp.exp(m - m_new)
    p = jnp.exp(s - m_new)
    l = alpha * l + jnp.sum(p, axis=-1, keepdims=True)
    acc = alpha * acc + jnp.dot(p.astype(BF16), v, preferred_element_type=F32)
    return m_new, l, acc


def _fox_attn_kernel(q_ref, k_ref, v_ref, ft_ref, fc_ref, o_ref, *, tq):
    qi = pl.program_id(2)
    q = q_ref[...]
    lo = _half_masks(tq)
    row = lax.broadcasted_iota(jnp.int32, (tq, tq), 0)
    col = lax.broadcasted_iota(jnp.int32, (tq, tq), 1)
    causal = col <= row
    outs = []
    for hh in range(2):
        qm = jnp.where(lo if hh == 0 else jnp.logical_not(lo), q, jnp.zeros_like(q))
        fq = fc_ref[0, 0, :, hh:hh + 1]

        def blk(kj, carry, masked):
            koff = pl.multiple_of(kj * tq, tq)
            k = k_ref[pl.ds(koff, tq), :]
            v = v_ref[pl.ds(koff, tq), :]
            s = lax.dot_general(qm, k, _NT, preferred_element_type=F32)
            fk = ft_ref[0, 0, hh:hh + 1, pl.ds(koff, tq)]
            s = s + (fq - fk)
            if masked:
                s = jnp.where(causal, s, NEG)
            return _online_softmax_step(s, v, *carry)

        init = (jnp.full((tq, 1), NEG, F32), jnp.zeros((tq, 1), F32), jnp.zeros((tq, LANES), F32))
        carry = lax.fori_loop(0, qi, lambda kj, c: blk(kj, c, False), init)
        _, l, acc = blk(qi, carry, True)
        outs.append(acc / l)
    o_ref[...] = jnp.where(lo, outs[0], outs[1]).astype(BF16)


def _diff_attn_kernel(tbl_ref, q_ref, k_ref, v_ref, bias_ref, lam_ref, g_ref, o_ref, *, tq, lam_init):
    h = pl.program_id(1)
    qi = pl.program_id(2)
    q = q_ref[...]
    lo = _half_masks(tq)
    row = lax.broadcasted_iota(jnp.int32, (tq, tq), 0)
    col = lax.broadcasted_iota(jnp.int32, (tq, tq), 1)
    causal = col <= row
    qc = (jnp.where(lo, q, jnp.zeros_like(q)), jnp.where(lo, jnp.zeros_like(q), q))
    far = tbl_ref[N_BUCKETS - 1, h]
    lp = lam_ref[...]
    lam = (jnp.exp(jnp.sum(lp[0:1] * lp[1:2], axis=-1, keepdims=True))
           - jnp.exp(jnp.sum(lp[2:3] * lp[3:4], axis=-1, keepdims=True)) + lam_init)

    def blk(kj, carry, masked):
        koff = pl.multiple_of(kj * tq, tq)
        k = k_ref[pl.ds(koff, tq), :]
        v = v_ref[pl.ds(koff, tq), :]
        if masked:
            bias = bias_ref[0, 0]
        else:
            bias = jnp.where(kj == qi - 1, bias_ref[0, 1], far)
        new = []
        for c in range(2):
            s = lax.dot_general(qc[c], k, _NT, preferred_element_type=F32) + bias
            if masked:
                s = jnp.where(causal, s, NEG)
            new.extend(_online_softmax_step(s, v, *carry[3 * c:3 * c + 3]))
        return tuple(new)

    one = (jnp.full((tq, 1), NEG, F32), jnp.zeros((tq, 1), F32), jnp.zeros((tq, LANES), F32))
    carry = lax.fori_loop(0, qi, lambda kj, c: blk(kj, c, False), one + one)
    _, l0, a0, _, l1, a1 = blk(qi, carry, True)
    o = a0 / l0 - lam * (a1 / l1)
    o_ref[...] = (_rmsnorm(o, g_ref[...]) * (1.0 - lam_init)).astype(BF16)


def _strict_upper_sum_matrix(n):
    j = lax.broadcasted_iota(jnp.int32, (n, n), 0)
    s = lax.broadcasted_iota(jnp.int32, (n, n), 1)
    return j > s


def _suffix_sums_2pass(x, mtri):
    hi = x.astype(BF16)
    lo = (x - hi.astype(F32)).astype(BF16)
    return (jnp.dot(hi, mtri, preferred_element_type=F32) + jnp.dot(lo, mtri, preferred_element_type=F32))


def _sb_attn_kernel(q_ref, k_ref, v_ref, o_ref, *, tq):
    qi = pl.program_id(2)
    q = q_ref[...]
    lo = _half_masks(tq)
    row = lax.broadcasted_iota(jnp.int32, (tq, tq), 0)
    col = lax.broadcasted_iota(jnp.int32, (tq, tq), 1)
    strict = col < row
    mtri = jnp.where(_strict_upper_sum_matrix(tq), 1.0, 0.0).astype(BF16)
    outs = []
    for hh in range(2):
        qm = jnp.where(lo if hh == 0 else jnp.logical_not(lo), q, jnp.zeros_like(q))

        def blk(kj, carry, masked):
            run, acc = carry
            koff = pl.multiple_of(kj * tq, tq)
            k = k_ref[pl.ds(koff, tq), :]
            v = v_ref[pl.ds(koff, tq), :]
            z = lax.dot_general(qm, k, _NT, preferred_element_type=F32)
            lsz = _log_sigmoid(z)
            lrest = lsz - z
            if masked:
                lrest = jnp.where(strict, lrest, 0.0)
            cs = _suffix_sums_2pass(lrest, mtri)
            a = jnp.exp(lsz + cs + run)
            if masked:
                a = jnp.where(strict, a, 0.0)
            acc = acc + jnp.dot(a.astype(BF16), v, preferred_element_type=F32)
            run = run + cs[:, 0:1] + lrest[:, 0:1]
            return run, acc

        carry = blk(qi, (jnp.zeros((tq, 1), F32), jnp.zeros((tq, LANES), F32)), True)
        _, acc = lax.fori_loop(0, qi, lambda t, c: blk(qi - 1 - t, c, False), carry)
        outs.append(acc)
    o_ref[...] = jnp.where(lo, outs[0], outs[1]).astype(BF16)


def _prompt_attn_specs(b, t, tq, q_blk, k_blk, v_blk):
    nq = t // tq
    q_spec = pl.BlockSpec((tq, LANES), lambda bi, g, qi: (bi * nq + qi, q_blk + g))
    k_spec = pl.BlockSpec((t, LANES), lambda bi, g, qi: (bi, k_blk + g))
    v_spec = pl.BlockSpec((t, LANES), lambda bi, g, qi: (bi, v_blk + g))
    o_spec = pl.BlockSpec((tq, LANES), lambda bi, g, qi: (bi * nq + qi, g))
    return nq, q_spec, k_spec, v_spec, o_spec


def _fox_attn(qkv, ft, fc, b, t, tq, q_blk, k_blk, v_blk):
    groups = H_FOX // 2
    nq, q_spec, k_spec, v_spec, o_spec = _prompt_attn_specs(b, t, tq, q_blk, k_blk, v_blk)
    return pl.pallas_call(
        functools.partial(_fox_attn_kernel, tq=tq),
        grid=(b, groups, nq),
        in_specs=[q_spec, k_spec, v_spec,
                  pl.BlockSpec((1, 1, 2, t), lambda bi, g, qi: (bi, g, 0, 0)),
                  pl.BlockSpec((1, 1, tq, 2), lambda bi, g, qi: (bi, g, qi, 0))],
        out_specs=o_spec,
        out_shape=jax.ShapeDtypeStruct((b * t, groups * LANES), BF16),
        compiler_params=_cparams("parallel", "parallel", "arbitrary"),
        name="fox_attn",
    )(qkv, qkv, qkv, ft, fc)


def _diff_attn(qkv, table, bias_blk, lam_p, subln_g, lam_init, b, t, tq, q_blk, k_blk, v_blk):
    nq, q_spec, k_spec, v_spec, o_spec = _prompt_attn_specs(b, t, tq, q_blk, k_blk, v_blk)
    return pl.pallas_call(
        functools.partial(_diff_attn_kernel, tq=tq, lam_init=lam_init),
        grid=(b, H_DIFF, nq),
        in_specs=[pl.BlockSpec(memory_space=pltpu.SMEM), q_spec, k_spec, v_spec,
                  pl.BlockSpec((1, 2, tq, tq), lambda bi, g, qi: (g, 0, 0, 0)),
                  pl.BlockSpec(lam_p.shape, lambda bi, g, qi: (0, 0)),
                  pl.BlockSpec((1, LANES), lambda bi, g, qi: (0, 0))],
        out_specs=o_spec,
        out_shape=jax.ShapeDtypeStruct((b * t, H_DIFF * LANES), BF16),
        compiler_params=_cparams("parallel", "parallel", "arbitrary"),
        name="diff_attn",
    )(table, qkv, qkv, qkv, bias_blk, lam_p, subln_g.reshape(1, LANES))


def _sb_attn(qkv, n_heads, b, t, tq, q_blk, k_blk, v_blk):
    groups = n_heads // 2
    nq, q_spec, k_spec, v_spec, o_spec = _prompt_attn_specs(b, t, tq, q_blk, k_blk, v_blk)
    return pl.pallas_call(
        functools.partial(_sb_attn_kernel, tq=tq),
        grid=(b, groups, nq),
        in_specs=[q_spec, k_spec, v_spec],
        out_specs=o_spec,
        out_shape=jax.ShapeDtypeStruct((b * t, groups * LANES), BF16),
        compiler_params=_cparams("parallel", "parallel", "arbitrary"),
        name="sb_attn",
    )(qkv, qkv, qkv)


def _head_mask(n_heads, width, group):
    hrow = lax.broadcasted_iota(jnp.int32, (n_heads, width), 0)
    hcol = lax.broadcasted_iota(jnp.int32, (n_heads, width), 1) // group
    return hrow == hcol


def _build_wq(q, n_heads):
    n_q, w = q.shape
    mask = _head_mask(n_heads, w, HEAD_DIM)
    parts = [jnp.where(mask, jnp.broadcast_to(q[i:i + 1, :], (n_heads, w)), 0.0) for i in range(n_q)]
    return jnp.concatenate(parts, axis=0).astype(BF16)


def _query_of_row(n_rows, n_heads, width):
    return lax.broadcasted_iota(jnp.int32, (n_rows, width), 0) // n_heads


def _fill_new_page(new_sc, kvnew_ref):
    new_sc[...] = jnp.zeros_like(new_sc)
    new_sc[0:kvnew_ref.shape[1], :] = kvnew_ref[0]


def _dec_softmax_update(s, vs, m_sc, l_sc, acc_sc):
    m_old = m_sc[...]
    m_new = jnp.maximum(m_old, jnp.max(s, axis=-1, keepdims=True))
    alpha = jnp.exp(m_old - m_new)
    p = jnp.exp(s - m_new)
    l_sc[...] = alpha * l_sc[...] + jnp.sum(p, axis=-1, keepdims=True)
    pb = p.astype(BF16)
    acc = alpha * acc_sc[...]
    for g, v in enumerate(vs):
        acc = acc + jnp.dot(pb[:, g * LANES:(g + 1) * LANES], v, preferred_element_type=F32)
    acc_sc[...] = acc
    m_sc[...] = m_new


def _gather_heads(acc, n_q, n_heads, mask):
    rows = [jnp.sum(jnp.where(mask, acc[i * n_heads:(i + 1) * n_heads], 0.0), axis=0, keepdims=True)
            for i in range(n_q)]
    return jnp.concatenate(rows, axis=0)


def _fox_dec_kernel(pt_ref, q_ref, kvnew_ref, lfn_ref, *refs, n_pages):
    del pt_ref
    pages = refs[:n_pages]
    lfs = refs[n_pages:2 * n_pages]
    o_ref = refs[2 * n_pages]
    wq_sc, m_sc, l_sc, acc_sc, car_sc, fq_sc, new_sc = refs[2 * n_pages + 1:]
    p = pl.program_id(1)
    n_q, w = q_ref.shape[1], q_ref.shape[2]
    n_rows = n_q * H_FOX
    page = pages[0].shape[0]

    @pl.when(p == 0)
    def _():
        wq = _build_wq(q_ref[0], H_FOX)
        wq_sc[...] = wq
        lfn = lfn_ref[0]
        run = jnp.zeros((H_FOX, 1), F32)
        cums = []
        for i in range(n_q):
            run = run + lfn[:, i:i + 1]
            cums.append(run)
        fq = jnp.concatenate(cums, axis=0)
        fq_sc[...] = fq
        lane = lax.broadcasted_iota(jnp.int32, (H_FOX, page), 1)
        fnew = jnp.zeros((H_FOX, page), F32)
        for j in range(n_q):
            fnew = jnp.where(lane == j, cums[j], fnew)
        fnew = jnp.concatenate([fnew] * n_q, axis=0)
        _fill_new_page(new_sc, kvnew_ref)
        kn = new_sc[:, :w].astype(BF16)
        vn = new_sc[:, w:].astype(BF16)
        s = lax.dot_general(wq, kn, _NT, preferred_element_type=F32) + (fq - fnew)
        jj = lax.broadcasted_iota(jnp.int32, (n_rows, page), 1)
        s = jnp.where(jj <= _query_of_row(n_rows, H_FOX, page), s, NEG)
        m_sc[...] = jnp.full_like(m_sc, NEG)
        l_sc[...] = jnp.zeros_like(l_sc)
        acc_sc[...] = jnp.zeros_like(acc_sc)
        car_sc[...] = jnp.zeros_like(car_sc)
        _dec_softmax_update(s, [vn], m_sc, l_sc, acc_sc)

    wq = wq_sc[...]
    mtri = jnp.where(_strict_upper_sum_matrix(page), 1.0, 0.0).astype(F32)
    lfx = jnp.concatenate([lf[...] for lf in lfs], axis=0)
    sfx = jnp.dot(lfx, mtri, preferred_element_type=F32, precision=lax.Precision.HIGHEST)
    carry = car_sc[...]
    s_parts = []
    for g in range(n_pages):
        sl = slice(g * H_FOX, (g + 1) * H_FOX)
        suf = sfx[sl] + carry
        carry = carry + sfx[sl, 0:1] + lfx[sl, 0:1]
        kp = pages[g][:, :w].astype(BF16)
        s = lax.dot_general(wq, kp, _NT, preferred_element_type=F32)
        s_parts.append(s + jnp.concatenate([suf] * n_q, axis=0))
    car_sc[...] = carry
    s = jnp.concatenate(s_parts, axis=1) + fq_sc[...]
    _dec_softmax_update(s, [pg[:, w:].astype(BF16) for pg in pages], m_sc, l_sc, acc_sc)

    @pl.when(p == pl.num_programs(1) - 1)
    def _():
        accn = acc_sc[...] / l_sc[...]
        o_ref[0] = _gather_heads(accn, n_q, H_FOX, _head_mask(H_FOX, w, HEAD_DIM))


def _diff_dec_kernel(pt_ref, q_ref, kvnew_ref, dec_ref, lam_ref, g_ref, *refs, n_pages, lam_init):
    del pt_ref
    pages = refs[:n_pages]
    o_ref = refs[n_pages]
    wq_sc, m_sc, l_sc, acc_sc, new_sc = refs[n_pages + 1:]
    p = pl.program_id(1)
    n_q, w = q_ref.shape[1], q_ref.shape[2]
    n_hc = 2 * H_DIFF
    n_rows = n_q * n_hc
    page = pages[0].shape[0]

    @pl.when(p == 0)
    def _():
        wq = _build_wq(q_ref[0], n_hc)
        wq_sc[...] = wq
        _fill_new_page(new_sc, kvnew_ref)
        kn = new_sc[:, :w].astype(BF16)
        vn = new_sc[:, w:].astype(BF16)
        s = lax.dot_general(wq, kn, _NT, preferred_element_type=F32) + dec_ref[1]
        jj = lax.broadcasted_iota(jnp.int32, (n_rows, page), 1)
        s = jnp.where(jj <= _query_of_row(n_rows, n_hc, page), s, NEG)
        m_sc[...] = jnp.full_like(m_sc, NEG)
        l_sc[...] = jnp.zeros_like(l_sc)
        acc_sc[...] = jnp.zeros_like(acc_sc)
        _dec_softmax_update(s, [vn], m_sc, l_sc, acc_sc)

    wq = wq_sc[...]
    far = dec_ref[2]
    s_parts = []
    for g in range(n_pages):
        kp = pages[g][:, :w].astype(BF16)
        s = lax.dot_general(wq, kp, _NT, preferred_element_type=F32)
        bias = jnp.where(p == 0, dec_ref[0], far) if g == 0 else far
        s_parts.append(s + bias)
    s = jnp.concatenate(s_parts, axis=1)
    _dec_softmax_update(s, [pg[:, w:].astype(BF16) for pg in pages], m_sc, l_sc, acc_sc)

    @pl.when(p == pl.num_programs(1) - 1)
    def _():
        lp = lam_ref[...]
        lam = (jnp.exp(jnp.sum(lp[0:1] * lp[1:2], axis=-1, keepdims=True))
               - jnp.exp(jnp.sum(lp[2:3] * lp[3:4], axis=-1, keepdims=True)) + lam_init)
        comp = lax.broadcasted_iota(jnp.int32, (n_rows, 1), 0) % 2
        accn = acc_sc[...] / l_sc[...] * jnp.where(comp == 0, 1.0, -lam)
        hc_row = lax.broadcasted_iota(jnp.int32, (n_hc, w), 0) // 2
        v_col = lax.broadcasted_iota(jnp.int32, (n_hc, w), 1) // (2 * HEAD_DIM)
        o = _gather_heads(accn, n_q, n_hc, hc_row == v_col)
        outs = []
        for hd in range(H_DIFF):
            oh = o[:, hd * LANES:(hd + 1) * LANES]
            outs.append(_rmsnorm(oh, g_ref[...]) * (1.0 - lam_init))
        o_ref[0] = jnp.concatenate(outs, axis=1)


def _sb_dec_kernel(pt_ref, q_ref, kvnew_ref, *refs, n_pages, n_heads):
    del pt_ref
    pages = refs[:n_pages]
    o_ref = refs[n_pages]
    wq_sc, acc_sc, run_sc, new_sc = refs[n_pages + 1:]
    p = pl.program_id(1)
    n_q, w = q_ref.shape[1], q_ref.shape[2]
    n_rows = n_q * n_heads
    page = pages[0].shape[0]
    mtri = jnp.where(_strict_upper_sum_matrix(page), 1.0, 0.0).astype(BF16)

    def absorb(wq, kps, vps, strict):
        lszs, lrests = [], []
        for kp in kps:
            z = lax.dot_general(wq, kp, _NT, preferred_element_type=F32)
            lsz = _log_sigmoid(z)
            lrest = lsz - z
            if strict is not None:
                lrest = jnp.where(strict, lrest, 0.0)
            lszs.append(lsz)
            lrests.append(lrest)
        cs_all = _suffix_sums_2pass(jnp.concatenate(lrests, axis=0), mtri)
        run = run_sc[...]
        acc = acc_sc[...]
        for g, vp in enumerate(vps):
            cs = cs_all[g * n_rows:(g + 1) * n_rows]
            a = jnp.exp(lszs[g] + cs + run)
            if strict is not None:
                a = jnp.where(strict, a, 0.0)
            acc = acc + jnp.dot(a.astype(BF16), vp, preferred_element_type=F32)
            run = run + cs[:, 0:1] + lrests[g][:, 0:1]
        run_sc[...] = run
        acc_sc[...] = acc

    @pl.when(p == 0)
    def _():
        wq = _build_wq(q_ref[0], n_heads)
        wq_sc[...] = wq
        _fill_new_page(new_sc, kvnew_ref)
        acc_sc[...] = jnp.zeros_like(acc_sc)
        run_sc[...] = jnp.zeros_like(run_sc)
        jj = lax.broadcasted_iota(jnp.int32, (n_rows, page), 1)
        strict = jj < _query_of_row(n_rows, n_heads, page)
        absorb(wq, [new_sc[:, :w].astype(BF16)], [new_sc[:, w:].astype(BF16)], strict)

    absorb(wq_sc[...], [pg[:, :w].astype(BF16) for pg in pages],
           [pg[:, w:].astype(BF16) for pg in pages], None)

    @pl.when(p == pl.num_programs(1) - 1)
    def _():
        o_ref[0] = _gather_heads(acc_sc[...], n_q, n_heads, _head_mask(n_heads, w, HEAD_DIM))


def _page_specs(pool, layer, n_steps, n_pages):
    total = n_steps * n_pages
    specs = []
    for g in range(n_pages):
        specs.append(pl.BlockSpec(
            (None, None) + pool.shape[2:],
            lambda b, p, pt, g=g: (layer, pt[b, total - 1 - (p * n_pages + g)], 0, 0)))
    return specs


def _per_batch_spec(shape):
    return pl.BlockSpec((1,) + shape[1:], lambda b, p, pt: (b,) + (0,) * (len(shape) - 1))


def _const_spec(shape):
    return pl.BlockSpec(shape, lambda b, p, pt: (0,) * len(shape))


def _decode_call(kernel_fn, name, page_table, pool, layer, n_pages, batch_args, const_args, extra_pools,
                 out_w, scratch):
    db, n_tbl = page_table.shape
    n_steps = n_tbl // n_pages
    in_specs = [_per_batch_spec(a.shape) for a in batch_args] + [_const_spec(a.shape) for a in const_args]
    args = list(batch_args) + list(const_args)
    for pl_arr in (pool,) + tuple(extra_pools):
        in_specs += _page_specs(pl_arr, layer, n_steps, n_pages)
        args += [pl_arr] * n_pages
    n_q = batch_args[0].shape[1]
    return pl.pallas_call(
        kernel_fn,
        grid_spec=pltpu.PrefetchScalarGridSpec(
            num_scalar_prefetch=1, grid=(db, n_steps), in_specs=in_specs,
            out_specs=pl.BlockSpec((1, n_q, out_w), lambda b, p, pt: (b, 0, 0)),
            scratch_shapes=scratch),
        out_shape=jax.ShapeDtypeStruct((db, n_q, out_w), F32),
        compiler_params=_cparams("parallel", "arbitrary"),
        name=name,
    )(page_table, *args)


def _pages_per_step(n_tbl, target):
    return _pick_tile(n_tbl, target)


def _ab_layer(xp, xs, dims, page_table, cache_kv_a, cache_lf, cache_kv_b, layer, g, w_in, b_f, lam_p,
              subln_g, w_out, table, bias_blk, dec_bias, lam_init, tq):
    b, t, db, n_q = dims
    d = xp.shape[1]
    splits = [0, FOX_W, 2 * FOX_W, 3 * FOX_W, 3 * FOX_W + H_FOX, 3 * FOX_W + H_FOX + DIFF_W,
              3 * FOX_W + H_FOX + 2 * DIFF_W, 3 * FOX_W + H_FOX + 3 * DIFF_W]
    q_a, k_a, v_a, f_a, q_b, k_b, v_b = [w_in[:, splits[i]:splits[i + 1]] for i in range(7)]
    wq = jnp.concatenate([q_a, q_b], axis=1).astype(BF16)
    wkva = jnp.concatenate([k_a, v_a], axis=1).astype(BF16)
    wkvb = jnp.concatenate([k_b, v_b], axis=1).astype(BF16)
    wf = jnp.pad(f_a, ((0, 0), (0, F_PAD - H_FOX))).astype(BF16)
    bf = jnp.pad(b_f, (0, F_PAD - H_FOX)).reshape(1, F_PAD)
    wo_a = w_out[:FOX_W].astype(BF16)
    wo_b = w_out[FOX_W:].astype(BF16)
    nblk = lambda cols: cols // LANES
    qa0, qb0 = 0, nblk(FOX_W)
    ka0 = nblk(FOX_W + DIFF_W)
    va0 = ka0 + nblk(FOX_W)
    kb0 = va0 + nblk(FOX_W)
    vb0 = kb0 + nblk(DIFF_W)

    qkv, kva, kvb, lf = _proj(xp, g, wq, [wkva, wkvb], wf, bf, H_FOX)
    ft = _cumsum_lanes(lf.reshape(b, t, H_FOX).transpose(0, 2, 1))
    ft4 = ft.reshape(b, H_FOX // 2, 2, t)
    fc4 = ft4.transpose(0, 1, 3, 2)
    o_a = _fox_attn(qkv, ft4, fc4, b, t, tq, qa0, ka0, va0)
    o_b = _diff_attn(qkv, table, bias_blk, lam_p, subln_g, lam_init, b, t, tq, qb0, kb0, vb0)
    xp = _outproj(xp, [o_a, o_b], [wo_a, wo_b])

    qkv_s, kva_s, kvb_s, lf_s = _proj(xs, g, wq, [wkva, wkvb], wf, bf, H_FOX)
    q_s = qkv_s[:, :FOX_W + DIFF_W].astype(F32).reshape(db, n_q, FOX_W + DIFF_W)
    lfn = lf_s.reshape(db, n_q, H_FOX).transpose(0, 2, 1)
    n_pool, page = cache_kv_a.shape[1], cache_kv_a.shape[2]
    pool_a = cache_kv_a.reshape(cache_kv_a.shape[0], n_pool, page, 2 * FOX_W)
    pool_b = cache_kv_b.reshape(cache_kv_b.shape[0], n_pool, page, 2 * DIFF_W)
    pool_lf = cache_lf.transpose(0, 1, 3, 2)
    n_pages = _pages_per_step(page_table.shape[1], 8)
    rows_a = n_q * H_FOX
    o_a_s = _decode_call(
        functools.partial(_fox_dec_kernel, n_pages=n_pages), "fox_decode", page_table, pool_a, layer, n_pages,
        [q_s[:, :, :FOX_W], kva_s.reshape(db, n_q, 2 * FOX_W), lfn], [], [pool_lf], FOX_W,
        [pltpu.VMEM((rows_a, FOX_W), BF16), pltpu.VMEM((rows_a, 1), F32), pltpu.VMEM((rows_a, 1), F32),
         pltpu.VMEM((rows_a, FOX_W), F32), pltpu.VMEM((H_FOX, 1), F32), pltpu.VMEM((rows_a, 1), F32),
         pltpu.VMEM((page, 2 * FOX_W), F32)])
    rows_b = n_q * 2 * H_DIFF
    o_b_s = _decode_call(
        functools.partial(_diff_dec_kernel, n_pages=n_pages, lam_init=lam_init), "diff_decode", page_table,
        pool_b, layer, n_pages,
        [q_s[:, :, FOX_W:], kvb_s.reshape(db, n_q, 2 * DIFF_W)],
        [dec_bias, lam_p, subln_g.reshape(1, LANES)], [], DIFF_W,
        [pltpu.VMEM((rows_b, DIFF_W), BF16), pltpu.VMEM((rows_b, 1), F32), pltpu.VMEM((rows_b, 1), F32),
         pltpu.VMEM((rows_b, DIFF_W), F32), pltpu.VMEM((page, 2 * DIFF_W), F32)])
    xs = _outproj(xs, [o_a_s.reshape(db * n_q, FOX_W), o_b_s.reshape(db * n_q, DIFF_W)], [wo_a, wo_b])
    return xp, xs, (kva, lf, kvb), (kva_s, lf_s, kvb_s)


def _sb_layer(xp, xs, dims, page_table, cache_kv, layer, g, w_in, w_out, tq):
    b, t, db, n_q = dims
    d = xp.shape[1]
    width = w_in.shape[1] // 3
    n_heads = width // HEAD_DIM
    wq = w_in[:, :width].astype(BF16)
    wkv = w_in[:, width:].astype(BF16)
    wo = w_out.astype(BF16)
    nb = width // LANES

    qkv, kv = _proj(xp, g, wq, [wkv])
    o = _sb_attn(qkv, n_heads, b, t, tq, 0, nb, 2 * nb)
    xp = _outproj(xp, [o], [wo])

    qkv_s, kv_s = _proj(xs, g, wq, [wkv])
    q_s = qkv_s[:, :width].astype(F32).reshape(db, n_q, width)
    n_pool, page = cache_kv.shape[1], cache_kv.shape[2]
    pool = cache_kv.reshape(cache_kv.shape[0], n_pool, page, 2 * width)
    n_pages = _pages_per_step(page_table.shape[1], 4)
    rows = n_q * n_heads
    o_s = _decode_call(
        functools.partial(_sb_dec_kernel, n_pages=n_pages, n_heads=n_heads), "sb_decode", page_table, pool,
        layer, n_pages, [q_s, kv_s.reshape(db, n_q, 2 * width)], [], [], width,
        [pltpu.VMEM((rows, width), BF16), pltpu.VMEM((rows, width), F32), pltpu.VMEM((rows, 1), F32),
         pltpu.VMEM((page, 2 * width), F32)])
    xs = _outproj(xs, [o_s.reshape(db * n_q, width)], [wo])
    return xp, xs, kv, kv_s


def kernel(x_prompt, x_sample, cache_fox_kv, cache_fox_logf, cache_diff_kv, cache_sb_kv, page_table, rel_bias_table, norm_g, final_norm_g, w_ffn_up, w_ffn_down, w_in_ab, b_forget, diff_lambda, diff_subln_g, w_out_ab, w_in_sb, w_out_sb):
    b, t, d = x_prompt.shape
    db, n_q, _ = x_sample.shape
    depth = norm_g.shape[0]
    page = cache_fox_kv.shape[2]
    assert page >= MAX_DISTANCE and page % LANES == 0 and n_q <= page
    tq = _pick_tile(t, 256)
    assert tq >= MAX_DISTANCE
    dims = (b, t, db, n_q)
    xp = x_prompt.reshape(b * t, d)
    xs = x_sample.reshape(db * n_q, d)
    bias_blk, dec_bias = _bias_tables(rel_bias_table, tq, n_q, page)

    p_fox_kv, p_fox_lf, p_diff_kv, p_sb_kv = [], [], [], []
    s_fox_kv, s_fox_lf, s_diff_kv, s_sb_kv = [], [], [], []
    for i in range(depth):
        l = i // 2
        last = i == depth - 1
        wu = w_ffn_up[i].astype(BF16)
        wd = w_ffn_down[i].astype(BF16)
        xp = _ffn(xp, norm_g[i, 0], wu[0], wd[0])
        xs = _ffn(xs, norm_g[i, 0], wu[0], wd[0])
        if i % 2 == 0:
            lam_init = 0.8 - 0.6 * math.exp(-0.3 * i)
            xp, xs, (kva, lf, kvb), (kva_s, lf_s, kvb_s) = _ab_layer(
                xp, xs, dims, page_table, cache_fox_kv, cache_fox_logf, cache_diff_kv, l, norm_g[i, 1],
                w_in_ab[l], b_forget[l], diff_lambda[l], diff_subln_g[l], w_out_ab[l], rel_bias_table,
                bias_blk, dec_bias, lam_init, tq)
            p_fox_kv.append(kva.reshape(b, t, 2, H_FOX, HEAD_DIM))
            p_fox_lf.append(lf.reshape(b, t, H_FOX))
            p_diff_kv.append(kvb.reshape(b, t, 2, H_DIFF, 2 * HEAD_DIM))
            s_fox_kv.append(kva_s.reshape(db, n_q, 2, H_FOX, HEAD_DIM))
            s_fox_lf.append(lf_s.reshape(db, n_q, H_FOX))
            s_diff_kv.append(kvb_s.reshape(db, n_q, 2, H_DIFF, 2 * HEAD_DIM))
        else:
            xp, xs, kv, kv_s = _sb_layer(xp, xs, dims, page_table, cache_sb_kv, l, norm_g[i, 1],
                                         w_in_sb[l], w_out_sb[l], tq)
            n_heads = kv.shape[1] // (2 * HEAD_DIM)
            p_sb_kv.append(kv.reshape(b, t, 2, n_heads, HEAD_DIM))
            s_sb_kv.append(kv_s.reshape(db, n_q, 2, n_heads, HEAD_DIM))
        fg = final_norm_g if last else None
        xp = _ffn(xp, norm_g[i, 2], wu[1], wd[1], final_g=fg)
        xs = _ffn(xs, norm_g[i, 2], wu[1], wd[1], final_g=fg)
    return (xp.reshape(b, t, d), xs.reshape(db, n_q, d),
            jnp.stack(p_fox_kv), jnp.stack(p_fox_lf), jnp.stack(p_diff_kv), jnp.stack(p_sb_kv),
            jnp.stack(s_fox_kv), jnp.stack(s_fox_lf), jnp.stack(s_diff_kv), jnp.stack(s_sb_kv))
```

```python
import functools
import math

import jax
import jax.numpy as jnp
from jax import lax
from jax.experimental import pallas as pl
from jax.experimental.pallas import tpu as pltpu

F32 = jnp.float32
BF16 = jnp.bfloat16

HEAD_DIM = 64
H_FOX = 8
H_DIFF = 4
FOX_W = H_FOX * HEAD_DIM
DIFF_W = H_DIFF * 2 * HEAD_DIM
N_BUCKETS = 32
MAX_EXACT = N_BUCKETS // 2
MAX_DISTANCE = 128
RMS_EPS = 1e-5
NEG = -1e30
Q_SCALE = HEAD_DIM ** -0.5

LANES = 128
F_PAD = LANES
ATT_W = 4 * LANES
VMEM_LIMIT = 48 * 1024 * 1024

_NT = (((1,), (1,)), ((), ()))


def _cparams(*sem):
    return pltpu.CompilerParams(dimension_semantics=sem, vmem_limit_bytes=VMEM_LIMIT)


def _rmsnorm(x, g):
    return x * lax.rsqrt(jnp.mean(x * x, axis=-1, keepdims=True) + RMS_EPS) * g


def _log_sigmoid(x):
    return jnp.minimum(x, 0.0) - jnp.log(1.0 + jnp.exp(-jnp.abs(x)))


def _pick_tile(n, target):
    t = min(n, target)
    while n % t:
        t //= 2
    return t


def _ffn_kernel(x_ref, g_ref, wg_ref, wu_ref, wd_ref, *rest, final):
    if final:
        fg_ref, o_ref, h_ref, acc_ref = rest
    else:
        o_ref, h_ref, acc_ref = rest
    j = pl.program_id(1)

    @pl.when(j == 0)
    def _():
        h_ref[...] = _rmsnorm(x_ref[...], g_ref[...]).astype(BF16)
        acc_ref[...] = jnp.zeros_like(acc_ref)

    h = h_ref[...]
    gate = jnp.dot(h, wg_ref[...], preferred_element_type=F32)
    up = jnp.dot(h, wu_ref[...], preferred_element_type=F32)
    act = (gate * jax.nn.sigmoid(gate) * up).astype(BF16)
    acc_ref[...] += jnp.dot(act, wd_ref[...], preferred_element_type=F32)

    @pl.when(j == pl.num_programs(1) - 1)
    def _():
        y = x_ref[...] + 0.5 * acc_ref[...]
        if final:
            y = _rmsnorm(y, fg_ref[...])
        o_ref[...] = y


def _ffn(x, g, w_up, w_down, final_g=None, tm_target=512, tf_target=1408):
    m, d = x.shape
    d_ff = w_down.shape[0]
    tm = _pick_tile(m, tm_target)
    tf = _pick_tile(d_ff, tf_target)
    nf = d_ff // tf
    final = final_g is not None
    in_specs = [
        pl.BlockSpec((tm, d), lambda i, j: (i, 0)),
        pl.BlockSpec((1, d), lambda i, j: (0, 0)),
        pl.BlockSpec((d, tf), lambda i, j: (0, j)),
        pl.BlockSpec((d, tf), lambda i, j: (0, j + nf)),
        pl.BlockSpec((tf, d), lambda i, j: (j, 0)),
    ]
    args = [x, g.reshape(1, d), w_up, w_up, w_down]
    if final:
        in_specs.append(pl.BlockSpec((1, d), lambda i, j: (0, 0)))
        args.append(final_g.reshape(1, d))
    return pl.pallas_call(
        functools.partial(_ffn_kernel, final=final),
        grid=(m // tm, nf),
        in_specs=in_specs,
        out_specs=pl.BlockSpec((tm, d), lambda i, j: (i, 0)),
        out_shape=jax.ShapeDtypeStruct((m, d), F32),
        scratch_shapes=[pltpu.VMEM((tm, d), BF16), pltpu.VMEM((tm, d), F32)],
        compiler_params=_cparams("parallel", "arbitrary"),
        name="half_ffn",
    )(*args)


def _proj_kernel(x_ref, g_ref, wq_ref, *rest, n_kv, with_f):
    kv_w = rest[:n_kv]
    rest = rest[n_kv:]
    if with_f:
        wf_ref, bf_ref = rest[:2]
        rest = rest[2:]
    qkv_ref = rest[0]
    kv_o = rest[1:1 + n_kv]
    h = _rmsnorm(x_ref[...], g_ref[...]).astype(BF16)
    q = jnp.dot(h, wq_ref[...], preferred_element_type=F32) * Q_SCALE
    wq_cols = wq_ref.shape[1]
    qkv_ref[:, :wq_cols] = q.astype(BF16)
    off = wq_cols
    for w_ref, o_ref in zip(kv_w, kv_o):
        kv = jnp.dot(h, w_ref[...], preferred_element_type=F32)
        o_ref[...] = kv
        qkv_ref[:, off:off + kv.shape[1]] = kv.astype(BF16)
        off += kv.shape[1]
    if with_f:
        lf_ref = rest[1 + n_kv]
        f = jnp.dot(h, wf_ref[...], preferred_element_type=F32) + bf_ref[...]
        lf_ref[...] = _log_sigmoid(f)[:, :lf_ref.shape[1]]


def _proj(x, g, wq, kv_ws, wf=None, bf=None, n_f=0, tm_target=512):
    m, d = x.shape
    tm = _pick_tile(m, tm_target)
    with_f = wf is not None
    n_kv = len(kv_ws)
    tot = wq.shape[1] + sum(w.shape[1] for w in kv_ws)
    const = lambda i: (0, 0)
    row = lambda i: (i, 0)
    in_specs = [pl.BlockSpec((tm, d), row), pl.BlockSpec((1, d), const),
                pl.BlockSpec(wq.shape, const)]
    in_specs += [pl.BlockSpec(w.shape, const) for w in kv_ws]
    args = [x, g.reshape(1, d), wq, *kv_ws]
    out_shape = [jax.ShapeDtypeStruct((m, tot), BF16)]
    out_specs = [pl.BlockSpec((tm, tot), row)]
    for w in kv_ws:
        out_shape.append(jax.ShapeDtypeStruct((m, w.shape[1]), F32))
        out_specs.append(pl.BlockSpec((tm, w.shape[1]), row))
    if with_f:
        in_specs += [pl.BlockSpec(wf.shape, const), pl.BlockSpec(bf.shape, const)]
        args += [wf, bf]
        out_shape.append(jax.ShapeDtypeStruct((m, n_f), F32))
        out_specs.append(pl.BlockSpec((tm, n_f), row))
    return pl.pallas_call(
        functools.partial(_proj_kernel, n_kv=n_kv, with_f=with_f),
        grid=(m // tm,),
        in_specs=in_specs,
        out_specs=out_specs,
        out_shape=out_shape,
        compiler_params=_cparams("parallel"),
        name="in_proj",
    )(*args)


def _outproj_kernel(x_ref, *rest, n_parts):
    o_refs = rest[:n_parts]
    w_refs = rest[n_parts:2 * n_parts]
    y_ref = rest[2 * n_parts]
    acc = x_ref[...]
    for o_ref, w_ref in zip(o_refs, w_refs):
        acc = acc + jnp.dot(o_ref[...].astype(BF16), w_ref[...], preferred_element_type=F32)
    y_ref[...] = acc


def _outproj(x, parts, ws, tm_target=512):
    m, d = x.shape
    tm = _pick_tile(m, tm_target)
    row = lambda i: (i, 0)
    const = lambda i: (0, 0)
    in_specs = [pl.BlockSpec((tm, d), row)]
    in_specs += [pl.BlockSpec((tm, p.shape[1]), row) for p in parts]
    in_specs += [pl.BlockSpec(w.shape, const) for w in ws]
    return pl.pallas_call(
        functools.partial(_outproj_kernel, n_parts=len(parts)),
        grid=(m // tm,),
        in_specs=in_specs,
        out_specs=pl.BlockSpec((tm, d), row),
        out_shape=jax.ShapeDtypeStruct((m, d), F32),
        compiler_params=_cparams("parallel"),
        name="out_proj",
    )(x, *parts, *ws)


def _cumsum_lanes_kernel(x_ref, o_ref):
    x = x_ref[0]
    t = x.shape[1]
    lane = lax.broadcasted_iota(jnp.int32, x.shape, 1)
    d = 1
    while d < t:
        x = x + jnp.where(lane >= d, pltpu.roll(x, d, axis=1), 0.0)
        d *= 2
    o_ref[0] = x


def _cumsum_lanes(x):
    b, h, t = x.shape
    spec = pl.BlockSpec((1, h, t), lambda i: (i, 0, 0))
    return pl.pallas_call(
        _cumsum_lanes_kernel, grid=(b,), in_specs=[spec], out_specs=spec,
        out_shape=jax.ShapeDtypeStruct(x.shape, F32),
        compiler_params=_cparams("parallel"), name="logf_cumsum",
    )(x)


def _bucket(n):
    nf = jnp.maximum(n, MAX_EXACT).astype(F32)
    large = MAX_EXACT + (jnp.log(nf / MAX_EXACT) / math.log(MAX_DISTANCE / MAX_EXACT)
                         * (N_BUCKETS - MAX_EXACT)).astype(jnp.int32)
    return jnp.where(n < MAX_EXACT, n, jnp.minimum(large, N_BUCKETS - 1))


def _bias_kernel(tbl_ref, blk_ref, dec_ref, *, tq, n_q, page):
    h = pl.program_id(0)
    r = lax.broadcasted_iota(jnp.int32, (tq, tq), 0)
    c = lax.broadcasted_iota(jnp.int32, (tq, tq), 1)
    for which, base in enumerate((0, tq)):
        bucket = _bucket(jnp.maximum(r - c + base, 0))
        val = jnp.zeros((tq, tq), F32)
        for b in range(N_BUCKETS):
            val = jnp.where(bucket == b, tbl_ref[b, h], val)
        blk_ref[0, which] = val
    n_rows = n_q * 2 * H_DIFF
    rr = lax.broadcasted_iota(jnp.int32, (n_rows, page), 0)
    jj = lax.broadcasted_iota(jnp.int32, (n_rows, page), 1)
    qi = rr % n_q
    hd = rr // (2 * n_q)
    dists = (page + qi - jj, jnp.maximum(qi - jj, 0), jnp.full((n_rows, page), MAX_DISTANCE, jnp.int32))
    for which, dist in enumerate(dists):
        bucket = _bucket(dist)
        val = jnp.zeros((n_rows, page), F32)
        for b in range(N_BUCKETS):
            tb = jnp.zeros((n_rows, page), F32)
            for hh in range(H_DIFF):
                tb = jnp.where(hd == hh, tbl_ref[b, hh], tb)
            val = jnp.where(bucket == b, tb, val)
        dec_ref[which] = val


def _bias_tables(table, tq, n_q, page):
    n_rows = n_q * 2 * H_DIFF
    return pl.pallas_call(
        functools.partial(_bias_kernel, tq=tq, n_q=n_q, page=page),
        grid=(H_DIFF,),
        in_specs=[pl.BlockSpec(memory_space=pltpu.SMEM)],
        out_specs=[pl.BlockSpec((1, 2, tq, tq), lambda h: (h, 0, 0, 0)),
                   pl.BlockSpec((3, n_rows, page), lambda h: (0, 0, 0))],
        out_shape=[jax.ShapeDtypeStruct((H_DIFF, 2, tq, tq), F32),
                   jax.ShapeDtypeStruct((3, n_rows, page), F32)],
        compiler_params=_cparams("arbitrary"),
        name="rel_bias_tables",
    )(table)


def _half_mask(tq, upper):
    lane = lax.broadcasted_iota(jnp.int32, (tq, LANES), 1)
    return lane >= HEAD_DIM if upper else lane < HEAD_DIM


def _tile_iota(tq):
    row = lax.broadcasted_iota(jnp.int32, (tq, tq), 0)
    col = lax.broadcasted_iota(jnp.int32, (tq, tq), 1)
    return row, col


def _softmax_chain_step(s, v, idx, m_sc, l_sc, acc_sc):
    m_old = m_sc[idx]
    m_new = jnp.maximum(m_old, jnp.max(s, axis=-1, keepdims=True))
    alpha = jnp.exp(m_old - m_new)
    p = jnp.exp(s - m_new)
    l_sc[idx] = alpha * l_sc[idx] + jnp.sum(p, axis=-1, keepdims=True)
    acc_sc[idx] = alpha * acc_sc[idx] + jnp.dot(p.astype(BF16), v, preferred_element_type=F32)
    m_sc[idx] = m_new


def _init_softmax_state(m_sc, l_sc, acc_sc):
    m_sc[...] = jnp.full_like(m_sc, NEG)
    l_sc[...] = jnp.zeros_like(l_sc)
    acc_sc[...] = jnp.zeros_like(acc_sc)


def _fox_attn_kernel(q_ref, k_ref, v_ref, ft_ref, fc_ref, o_ref, m_sc, l_sc, acc_sc, *, tq):
    qi = pl.program_id(2)
    n_heads = ATT_W // HEAD_DIM
    row, col = _tile_iota(tq)
    _init_softmax_state(m_sc, l_sc, acc_sc)

    def blk(kj, masked):
        koff = pl.multiple_of(kj * tq, tq)
        for h in range(n_heads):
            pr = slice((h // 2) * LANES, (h // 2 + 1) * LANES)
            q = q_ref[:, pr]
            qm = jnp.where(_half_mask(tq, h % 2 == 1), q, jnp.zeros_like(q))
            k = k_ref[pl.ds(koff, tq), pr]
            v = v_ref[pl.ds(koff, tq), pr]
            s = lax.dot_general(qm, k, _NT, preferred_element_type=F32)
            s = s + (fc_ref[0, :, h:h + 1] - ft_ref[0, h:h + 1, pl.ds(koff, tq)])
            if masked:
                s = jnp.where(col <= row, s, NEG)
            _softmax_chain_step(s, v, h, m_sc, l_sc, acc_sc)

    def body(kj, c):
        blk(kj, False)
        return c

    lax.fori_loop(0, qi, body, 0)
    blk(qi, True)
    for p in range(n_heads // 2):
        o = jnp.where(_half_mask(tq, False), acc_sc[2 * p] / l_sc[2 * p], acc_sc[2 * p + 1] / l_sc[2 * p + 1])
        o_ref[:, p * LANES:(p + 1) * LANES] = o.astype(BF16)


def _diff_lambda(lp, lam_init):
    return (jnp.exp(jnp.sum(lp[0:1] * lp[1:2], axis=-1, keepdims=True))
            - jnp.exp(jnp.sum(lp[2:3] * lp[3:4], axis=-1, keepdims=True)) + lam_init)


def _diff_attn_kernel(tbl_ref, q_ref, k_ref, v_ref, bias_ref, lam_ref, g_ref, o_ref, m_sc, l_sc, acc_sc,
                      *, tq, lam_init):
    qi = pl.program_id(2)
    row, col = _tile_iota(tq)
    _init_softmax_state(m_sc, l_sc, acc_sc)

    def blk(kj, masked):
        koff = pl.multiple_of(kj * tq, tq)
        for h in range(H_DIFF):
            hs = slice(h * LANES, (h + 1) * LANES)
            q = q_ref[:, hs]
            k = k_ref[pl.ds(koff, tq), hs]
            v = v_ref[pl.ds(koff, tq), hs]
            if masked:
                bias = bias_ref[h, 0]
            else:
                bias = jnp.where(kj == qi - 1, bias_ref[h, 1], tbl_ref[N_BUCKETS - 1, h])
            for c in range(2):
                qc = jnp.where(_half_mask(tq, c == 1), q, jnp.zeros_like(q))
                s = lax.dot_general(qc, k, _NT, preferred_element_type=F32) + bias
                if masked:
                    s = jnp.where(col <= row, s, NEG)
                _softmax_chain_step(s, v, 2 * h + c, m_sc, l_sc, acc_sc)

    def body(kj, c):
        blk(kj, False)
        return c

    lax.fori_loop(0, qi, body, 0)
    blk(qi, True)
    lam = _diff_lambda(lam_ref[...], lam_init)
    for h in range(H_DIFF):
        o = acc_sc[2 * h] / l_sc[2 * h] - lam * (acc_sc[2 * h + 1] / l_sc[2 * h + 1])
        o_ref[:, h * LANES:(h + 1) * LANES] = (_rmsnorm(o, g_ref[...]) * (1.0 - lam_init)).astype(BF16)


def _strict_upper_sum_matrix(n):
    j = lax.broadcasted_iota(jnp.int32, (n, n), 0)
    s = lax.broadcasted_iota(jnp.int32, (n, n), 1)
    return j > s


def _suffix_sums_2pass(x, mtri):
    hi = x.astype(BF16)
    lo = (x - hi.astype(F32)).astype(BF16)
    return (jnp.dot(hi, mtri, preferred_element_type=F32) + jnp.dot(lo, mtri, preferred_element_type=F32))


def _sb_attn_kernel(q_ref, k_ref, v_ref, o_ref, run_sc, acc_sc, *, tq):
    qi = pl.program_id(2)
    n_heads = ATT_W // HEAD_DIM
    row, col = _tile_iota(tq)
    run_sc[...] = jnp.zeros_like(run_sc)
    acc_sc[...] = jnp.zeros_like(acc_sc)

    def blk(kj, masked):
        koff = pl.multiple_of(kj * tq, tq)
        mtri = jnp.where(_strict_upper_sum_matrix(tq), 1.0, 0.0).astype(BF16)
        for h in range(n_heads):
            pr = slice((h // 2) * LANES, (h // 2 + 1) * LANES)
            q = q_ref[:, pr]
            qm = jnp.where(_half_mask(tq, h % 2 == 1), q, jnp.zeros_like(q))
            k = k_ref[pl.ds(koff, tq), pr]
            v = v_ref[pl.ds(koff, tq), pr]
            z = lax.dot_general(qm, k, _NT, preferred_element_type=F32)
            lsz = _log_sigmoid(z)
            lrest = lsz - z
            if masked:
                lrest = jnp.where(col < row, lrest, 0.0)
            cs = _suffix_sums_2pass(lrest, mtri)
            run = run_sc[h]
            a = jnp.exp(lsz + cs + run)
            if masked:
                a = jnp.where(col < row, a, 0.0)
            acc_sc[h] = acc_sc[h] + jnp.dot(a.astype(BF16), v, preferred_element_type=F32)
            run_sc[h] = run + cs[:, 0:1] + lrest[:, 0:1]

    def body(t, c):
        blk(qi - 1 - t, False)
        return c

    blk(qi, True)
    lax.fori_loop(0, qi, body, 0)
    for p in range(n_heads // 2):
        o = jnp.where(_half_mask(tq, False), acc_sc[2 * p], acc_sc[2 * p + 1])
        o_ref[:, p * LANES:(p + 1) * LANES] = o.astype(BF16)


def _attn_specs(t, tq, q_blk, k_blk, v_blk):
    nq = t // tq
    q_spec = pl.BlockSpec((tq, ATT_W), lambda bi, g, qi: (bi * nq + qi, q_blk + g))
    k_spec = pl.BlockSpec((t, ATT_W), lambda bi, g, qi: (bi, k_blk + g))
    v_spec = pl.BlockSpec((t, ATT_W), lambda bi, g, qi: (bi, v_blk + g))
    o_spec = pl.BlockSpec((tq, ATT_W), lambda bi, g, qi: (bi * nq + qi, g))
    return nq, q_spec, k_spec, v_spec, o_spec


def _softmax_scratch(n_chains, tq):
    return [pltpu.VMEM((n_chains, tq, 1), F32), pltpu.VMEM((n_chains, tq, 1), F32),
            pltpu.VMEM((n_chains, tq, LANES), F32)]


_ATTN_SEM = ("parallel", "parallel", "arbitrary")


def _fox_attn(qkv, ft, fc, b, t, tq, q_blk, k_blk, v_blk):
    nq, q_spec, k_spec, v_spec, o_spec = _attn_specs(t, tq, q_blk, k_blk, v_blk)
    return pl.pallas_call(
        functools.partial(_fox_attn_kernel, tq=tq),
        grid=(b, FOX_W // ATT_W, nq),
        in_specs=[q_spec, k_spec, v_spec,
                  pl.BlockSpec((1, ATT_W // HEAD_DIM, t), lambda bi, g, qi: (bi, g, 0)),
                  pl.BlockSpec((1, tq, ATT_W // HEAD_DIM), lambda bi, g, qi: (bi, qi, g))],
        out_specs=o_spec,
        out_shape=jax.ShapeDtypeStruct((b * t, FOX_W), BF16),
        scratch_shapes=_softmax_scratch(ATT_W // HEAD_DIM, tq),
        compiler_params=_cparams(*_ATTN_SEM),
        name="fox_attn",
    )(qkv, qkv, qkv, ft, fc)


def _diff_attn(qkv, table, bias_blk, lam_p, subln_g, lam_init, b, t, tq, q_blk, k_blk, v_blk):
    nq, q_spec, k_spec, v_spec, o_spec = _attn_specs(t, tq, q_blk, k_blk, v_blk)
    return pl.pallas_call(
        functools.partial(_diff_attn_kernel, tq=tq, lam_init=lam_init),
        grid=(b, DIFF_W // ATT_W, nq),
        in_specs=[pl.BlockSpec(memory_space=pltpu.SMEM), q_spec, k_spec, v_spec,
                  pl.BlockSpec(bias_blk.shape, lambda bi, g, qi: (0, 0, 0, 0)),
                  pl.BlockSpec(lam_p.shape, lambda bi, g, qi: (0, 0)),
                  pl.BlockSpec((1, LANES), lambda bi, g, qi: (0, 0))],
        out_specs=o_spec,
        out_shape=jax.ShapeDtypeStruct((b * t, DIFF_W), BF16),
        scratch_shapes=_softmax_scratch(2 * H_DIFF, tq),
        compiler_params=_cparams(*_ATTN_SEM),
        name="diff_attn",
    )(table, qkv, qkv, qkv, bias_blk, lam_p, subln_g.reshape(1, LANES))


def _sb_attn(qkv, width, b, t, tq, q_blk, k_blk, v_blk):
    nq, q_spec, k_spec, v_spec, o_spec = _attn_specs(t, tq, q_blk, k_blk, v_blk)
    n_chains = ATT_W // HEAD_DIM
    return pl.pallas_call(
        functools.partial(_sb_attn_kernel, tq=tq),
        grid=(b, width // ATT_W, nq),
        in_specs=[q_spec, k_spec, v_spec],
        out_specs=o_spec,
        out_shape=jax.ShapeDtypeStruct((b * t, width), BF16),
        scratch_shapes=[pltpu.VMEM((n_chains, tq, 1), F32), pltpu.VMEM((n_chains, tq, LANES), F32)],
        compiler_params=_cparams(*_ATTN_SEM),
        name="sb_attn",
    )(qkv, qkv, qkv)


def _head_mask(n_heads, width, group):
    hrow = lax.broadcasted_iota(jnp.int32, (n_heads, width), 0)
    hcol = lax.broadcasted_iota(jnp.int32, (n_heads, width), 1) // group
    return hrow == hcol


def _build_wq(q, n_heads):
    n_q, w = q.shape
    mask = _head_mask(n_heads, w, HEAD_DIM)
    parts = [jnp.where(mask, jnp.broadcast_to(q[i:i + 1, :], (n_heads, w)), 0.0) for i in range(n_q)]
    return jnp.concatenate(parts, axis=0).astype(BF16)


def _query_of_row(n_rows, n_heads, width):
    return lax.broadcasted_iota(jnp.int32, (n_rows, width), 0) // n_heads


def _fill_new_page(new_sc, kvnew_ref):
    new_sc[...] = jnp.zeros_like(new_sc)
    new_sc[:, :, 0:kvnew_ref.shape[3]] = kvnew_ref[0]


def _dec_softmax_update(s, vts, m_sc, l_sc, acc_sc):
    page = vts[0].shape[1]
    m_old = m_sc[...]
    m_new = jnp.maximum(m_old, jnp.max(s, axis=-1, keepdims=True))
    alpha = jnp.exp(m_old - m_new)
    p = jnp.exp(s - m_new)
    l_sc[...] = alpha * l_sc[...] + jnp.sum(p, axis=-1, keepdims=True)
    pb = p.astype(BF16)
    acc = alpha * acc_sc[...]
    for g, vt in enumerate(vts):
        acc = acc + lax.dot_general(pb[:, g * page:(g + 1) * page], vt, _NT, preferred_element_type=F32)
    acc_sc[...] = acc
    m_sc[...] = m_new


def _gather_heads(acc, n_q, n_heads, mask):
    rows = [jnp.sum(jnp.where(mask, acc[i * n_heads:(i + 1) * n_heads], 0.0), axis=0, keepdims=True)
            for i in range(n_q)]
    return jnp.concatenate(rows, axis=0)


def _fox_dec_kernel(pt_ref, q_ref, kvnew_ref, lfn_ref, *refs, n_pages):
    del pt_ref
    pages = refs[:n_pages]
    lfs = refs[n_pages:2 * n_pages]
    o_ref = refs[2 * n_pages]
    wq_sc, m_sc, l_sc, acc_sc, car_sc, fq_sc, new_sc = refs[2 * n_pages + 1:]
    p = pl.program_id(1)
    n_q, w = q_ref.shape[1], q_ref.shape[2]
    n_rows = n_q * H_FOX
    page = pages[0].shape[2]

    @pl.when(p == 0)
    def _():
        wq = _build_wq(q_ref[0], H_FOX)
        wq_sc[...] = wq
        lfn = lfn_ref[0]
        run = jnp.zeros((H_FOX, 1), F32)
        cums = []
        for i in range(n_q):
            run = run + lfn[:, i:i + 1]
            cums.append(run)
        fq = jnp.concatenate(cums, axis=0)
        fq_sc[...] = fq
        lane = lax.broadcasted_iota(jnp.int32, (H_FOX, page), 1)
        fnew = jnp.zeros((H_FOX, page), F32)
        for j in range(n_q):
            fnew = jnp.where(lane == j, cums[j], fnew)
        fnew = jnp.concatenate([fnew] * n_q, axis=0)
        _fill_new_page(new_sc, kvnew_ref)
        s = jnp.dot(wq, new_sc[0].astype(BF16), preferred_element_type=F32) + (fq - fnew)
        jj = lax.broadcasted_iota(jnp.int32, (n_rows, page), 1)
        s = jnp.where(jj <= _query_of_row(n_rows, H_FOX, page), s, NEG)
        _init_softmax_state(m_sc, l_sc, acc_sc)
        car_sc[...] = jnp.zeros_like(car_sc)
        _dec_softmax_update(s, [new_sc[1].astype(BF16)], m_sc, l_sc, acc_sc)

    wq = wq_sc[...]
    mtri = jnp.where(_strict_upper_sum_matrix(page), 1.0, 0.0).astype(F32)
    lfx = jnp.concatenate([lf[...] for lf in lfs], axis=0)
    sfx = jnp.dot(lfx, mtri, preferred_element_type=F32, precision=lax.Precision.HIGHEST)
    carry = car_sc[...]
    s_parts = []
    for g in range(n_pages):
        sl = slice(g * H_FOX, (g + 1) * H_FOX)
        suf = sfx[sl] + carry
        carry = carry + sfx[sl, 0:1] + lfx[sl, 0:1]
        s = jnp.dot(wq, pages[g][0].astype(BF16), preferred_element_type=F32)
        s_parts.append(s + jnp.concatenate([suf] * n_q, axis=0))
    car_sc[...] = carry
    s = jnp.concatenate(s_parts, axis=1) + fq_sc[...]
    _dec_softmax_update(s, [pg[1].astype(BF16) for pg in pages], m_sc, l_sc, acc_sc)

    @pl.when(p == pl.num_programs(1) - 1)
    def _():
        accn = acc_sc[...] / l_sc[...]
        o_ref[0] = _gather_heads(accn, n_q, H_FOX, _head_mask(H_FOX, w, HEAD_DIM))


def _diff_dec_kernel(pt_ref, q_ref, kvnew_ref, dec_ref, lam_ref, g_ref, *refs, n_pages, lam_init):
    del pt_ref
    pages = refs[:n_pages]
    o_ref = refs[n_pages]
    m_sc, l_sc, acc_sc, new_sc = refs[n_pages + 1:]
    p = pl.program_id(1)
    n_q = o_ref.shape[1]
    rph = 2 * n_q
    n_rows = H_DIFF * rph
    stride = 2 * H_DIFF
    page = pages[0].shape[0] // stride

    def scores(ref):
        parts = []
        for h in range(H_DIFF):
            kh = ref[pl.ds(h, page, stride=stride), :].astype(BF16)
            qh = q_ref[0, h * rph:(h + 1) * rph, :].astype(BF16)
            parts.append(lax.dot_general(qh, kh, _NT, preferred_element_type=F32))
        return jnp.concatenate(parts, axis=0)

    def values(ref):
        return [ref[pl.ds(H_DIFF + h, page, stride=stride), :].astype(BF16) for h in range(H_DIFF)]

    def update(s, vals):
        m_old = m_sc[...]
        m_new = jnp.maximum(m_old, jnp.max(s, axis=-1, keepdims=True))
        alpha = jnp.exp(m_old - m_new)
        pr = jnp.exp(s - m_new)
        l_sc[...] = alpha * l_sc[...] + jnp.sum(pr, axis=-1, keepdims=True)
        acc = alpha * acc_sc[...]
        rows = []
        for h in range(H_DIFF):
            rs = slice(h * rph, (h + 1) * rph)
            a = acc[rs]
            for g, vs in enumerate(vals):
                a = a + jnp.dot(pr[rs, g * page:(g + 1) * page].astype(BF16), vs[h],
                                preferred_element_type=F32)
            rows.append(a)
        acc_sc[...] = jnp.concatenate(rows, axis=0)
        m_sc[...] = m_new

    @pl.when(p == 0)
    def _():
        new_sc[...] = jnp.zeros_like(new_sc)
        new_sc[0:kvnew_ref.shape[1], :] = kvnew_ref[0]
        s = scores(new_sc) + dec_ref[1]
        jj = lax.broadcasted_iota(jnp.int32, (n_rows, page), 1)
        ii = lax.broadcasted_iota(jnp.int32, (n_rows, page), 0) % n_q
        s = jnp.where(jj <= ii, s, NEG)
        _init_softmax_state(m_sc, l_sc, acc_sc)
        update(s, [values(new_sc)])

    far = dec_ref[2]
    s_parts = []
    for g in range(n_pages):
        bias = jnp.where(p == 0, dec_ref[0], far) if g == 0 else far
        s_parts.append(scores(pages[g]) + bias)
    update(jnp.concatenate(s_parts, axis=1), [values(pg) for pg in pages])

    @pl.when(p == pl.num_programs(1) - 1)
    def _():
        lam = _diff_lambda(lam_ref[...], lam_init)
        accn = acc_sc[...] / l_sc[...]
        outs = []
        for h in range(H_DIFF):
            blk = accn[h * rph:(h + 1) * rph]
            oh = blk[0:n_q] - lam * blk[n_q:rph]
            outs.append(_rmsnorm(oh, g_ref[...]) * (1.0 - lam_init))
        o_ref[0] = jnp.concatenate(outs, axis=1)


def _sb_dec_kernel(pt_ref, q_ref, kvnew_ref, *refs, n_pages, n_heads):
    del pt_ref
    pages = refs[:n_pages]
    o_ref = refs[n_pages]
    wq_sc, acc_sc, run_sc, new_sc = refs[n_pages + 1:]
    p = pl.program_id(1)
    n_q, w = q_ref.shape[1], q_ref.shape[2]
    n_rows = n_q * n_heads
    page = pages[0].shape[2]

    def absorb(wq, kts, vts, strict):
        mtri = jnp.where(_strict_upper_sum_matrix(page), 1.0, 0.0).astype(BF16)
        lszs, lrests = [], []
        for kt in kts:
            z = jnp.dot(wq, kt, preferred_element_type=F32)
            lsz = _log_sigmoid(z)
            lrest = lsz - z
            if strict is not None:
                lrest = jnp.where(strict, lrest, 0.0)
            lszs.append(lsz)
            lrests.append(lrest)
        cs_all = _suffix_sums_2pass(jnp.concatenate(lrests, axis=0), mtri)
        run = run_sc[...]
        acc = acc_sc[...]
        for g, vt in enumerate(vts):
            cs = cs_all[g * n_rows:(g + 1) * n_rows]
            a = jnp.exp(lszs[g] + cs + run)
            if strict is not None:
                a = jnp.where(strict, a, 0.0)
            acc = acc + lax.dot_general(a.astype(BF16), vt, _NT, preferred_element_type=F32)
            run = run + cs[:, 0:1] + lrests[g][:, 0:1]
        run_sc[...] = run
        acc_sc[...] = acc

    @pl.when(p == 0)
    def _():
        wq = _build_wq(q_ref[0], n_heads)
        wq_sc[...] = wq
        _fill_new_page(new_sc, kvnew_ref)
        acc_sc[...] = jnp.zeros_like(acc_sc)
        run_sc[...] = jnp.zeros_like(run_sc)
        jj = lax.broadcasted_iota(jnp.int32, (n_rows, page), 1)
        strict = jj < _query_of_row(n_rows, n_heads, page)
        absorb(wq, [new_sc[0].astype(BF16)], [new_sc[1].astype(BF16)], strict)

    absorb(wq_sc[...], [pg[0].astype(BF16) for pg in pages], [pg[1].astype(BF16) for pg in pages], None)

    @pl.when(p == pl.num_programs(1) - 1)
    def _():
        o_ref[0] = _gather_heads(acc_sc[...], n_q, n_heads, _head_mask(n_heads, w, HEAD_DIM))


def _page_specs(pool, layer, n_steps, n_pages):
    total = n_steps * n_pages
    tail = (0,) * (pool.ndim - 2)
    specs = []
    for g in range(n_pages):
        specs.append(pl.BlockSpec(
            (None, None) + pool.shape[2:],
            lambda b, p, pt, g=g: (layer, pt[b, total - 1 - (p * n_pages + g)]) + tail))
    return specs


def _per_batch_spec(shape):
    return pl.BlockSpec((1,) + shape[1:], lambda b, p, pt: (b,) + (0,) * (len(shape) - 1))


def _const_spec(shape):
    return pl.BlockSpec(shape, lambda b, p, pt: (0,) * len(shape))


def _decode_call(kernel_fn, name, page_table, pool, layer, n_pages, batch_args, const_args, extra_pools,
                 n_q, out_w, scratch):
    db, n_tbl = page_table.shape
    n_steps = n_tbl // n_pages
    in_specs = [_per_batch_spec(a.shape) for a in batch_args] + [_const_spec(a.shape) for a in const_args]
    args = list(batch_args) + list(const_args)
    for pl_arr in (pool,) + tuple(extra_pools):
        in_specs += _page_specs(pl_arr, layer, n_steps, n_pages)
        args += [pl_arr] * n_pages
    return pl.pallas_call(
        kernel_fn,
        grid_spec=pltpu.PrefetchScalarGridSpec(
            num_scalar_prefetch=1, grid=(db, n_steps), in_specs=in_specs,
            out_specs=pl.BlockSpec((1, n_q, out_w), lambda b, p, pt: (b, 0, 0)),
            scratch_shapes=scratch),
        out_shape=jax.ShapeDtypeStruct((db, n_q, out_w), F32),
        compiler_params=_cparams("parallel", "arbitrary"),
        name=name,
    )(page_table, *args)


def _transposed_pool(cache):
    l, n_pool, page, two, h, dd = cache.shape
    return cache.transpose(0, 1, 3, 4, 5, 2).reshape(l, n_pool, two, h * dd, page)


def _transposed_new(kv_s, db, n_q, width):
    return kv_s.reshape(db, n_q, 2, width).transpose(0, 2, 3, 1)


def _ab_layer(xp, xs, dims, page_table, cache_kv_a, cache_lf, cache_kv_b, layer, g, w_in, b_f, lam_p,
              subln_g, w_out, table, bias_blk, dec_bias, lam_init, tq):
    b, t, db, n_q = dims
    splits = [0, FOX_W, 2 * FOX_W, 3 * FOX_W, 3 * FOX_W + H_FOX, 3 * FOX_W + H_FOX + DIFF_W,
              3 * FOX_W + H_FOX + 2 * DIFF_W, 3 * FOX_W + H_FOX + 3 * DIFF_W]
    q_a, k_a, v_a, f_a, q_b, k_b, v_b = [w_in[:, splits[i]:splits[i + 1]] for i in range(7)]
    wq = jnp.concatenate([q_a, q_b], axis=1).astype(BF16)
    wkva = jnp.concatenate([k_a, v_a], axis=1).astype(BF16)
    wkvb = jnp.concatenate([k_b, v_b], axis=1).astype(BF16)
    wf = jnp.pad(f_a, ((0, 0), (0, F_PAD - H_FOX))).astype(BF16)
    bf = jnp.pad(b_f, (0, F_PAD - H_FOX)).reshape(1, F_PAD)
    wo_a = w_out[:FOX_W].astype(BF16)
    wo_b = w_out[FOX_W:].astype(BF16)
    qa0, qb0, ka0, va0, kb0, vb0 = range(6)

    qkv, kva, kvb, lf = _proj(xp, g, wq, [wkva, wkvb], wf, bf, H_FOX)
    ft = _cumsum_lanes(lf.reshape(b, t, H_FOX).transpose(0, 2, 1))
    fc = ft.transpose(0, 2, 1)
    o_a = _fox_attn(qkv, ft, fc, b, t, tq, qa0, ka0, va0)
    o_b = _diff_attn(qkv, table, bias_blk, lam_p, subln_g, lam_init, b, t, tq, qb0, kb0, vb0)
    xp = _outproj(xp, [o_a, o_b], [wo_a, wo_b])

    qkv_s, kva_s, kvb_s, lf_s = _proj(xs, g, wq, [wkva, wkvb], wf, bf, H_FOX)
    q_s = qkv_s[:, :FOX_W + DIFF_W].astype(F32).reshape(db, n_q, FOX_W + DIFF_W)
    lfn = lf_s.reshape(db, n_q, H_FOX).transpose(0, 2, 1)
    n_pool, page = cache_kv_a.shape[1], cache_kv_a.shape[2]
    pool_a = _transposed_pool(cache_kv_a)
    pool_lf = cache_lf.transpose(0, 1, 3, 2)
    pool_b = cache_kv_b.reshape(cache_kv_b.shape[0], n_pool, page * 2 * H_DIFF, 2 * HEAD_DIM)
    n_pages = _pick_tile(page_table.shape[1], 8)
    rows_a = n_q * H_FOX
    o_a_s = _decode_call(
        functools.partial(_fox_dec_kernel, n_pages=n_pages), "fox_decode", page_table, pool_a, layer, n_pages,
        [q_s[:, :, :FOX_W], _transposed_new(kva_s, db, n_q, FOX_W), lfn], [], [pool_lf], n_q, FOX_W,
        [pltpu.VMEM((rows_a, FOX_W), BF16), pltpu.VMEM((rows_a, 1), F32), pltpu.VMEM((rows_a, 1), F32),
         pltpu.VMEM((rows_a, FOX_W), F32), pltpu.VMEM((H_FOX, 1), F32), pltpu.VMEM((rows_a, 1), F32),
         pltpu.VMEM((2, FOX_W, page), F32)])
    qb = q_s[:, :, FOX_W:].reshape(db, n_q, H_DIFF, 2, HEAD_DIM).transpose(0, 2, 3, 1, 4)
    q_rows = (qb[:, :, :, :, None, :] * jnp.eye(2, dtype=F32)[None, None, :, None, :, None]).reshape(
        db, H_DIFF * 2 * n_q, 2 * HEAD_DIM)
    rows_b = n_q * 2 * H_DIFF
    o_b_s = _decode_call(
        functools.partial(_diff_dec_kernel, n_pages=n_pages, lam_init=lam_init), "diff_decode", page_table,
        pool_b, layer, n_pages,
        [q_rows, kvb_s.reshape(db, n_q * 2 * H_DIFF, 2 * HEAD_DIM)],
        [dec_bias, lam_p, subln_g.reshape(1, LANES)], [], n_q, DIFF_W,
        [pltpu.VMEM((rows_b, 1), F32), pltpu.VMEM((rows_b, 1), F32), pltpu.VMEM((rows_b, 2 * HEAD_DIM), F32),
         pltpu.VMEM((page * 2 * H_DIFF, 2 * HEAD_DIM), F32)])
    xs = _outproj(xs, [o_a_s.reshape(db * n_q, FOX_W), o_b_s.reshape(db * n_q, DIFF_W)], [wo_a, wo_b])
    return xp, xs, (kva, lf, kvb), (kva_s, lf_s, kvb_s)


def _sb_layer(xp, xs, dims, page_table, cache_kv, layer, g, w_in, w_out, tq):
    b, t, db, n_q = dims
    width = w_in.shape[1] // 3
    n_heads = width // HEAD_DIM
    wq = w_in[:, :width].astype(BF16)
    wkv = w_in[:, width:].astype(BF16)
    wo = w_out.astype(BF16)
    nb = width // ATT_W

    qkv, kv = _proj(xp, g, wq, [wkv])
    o = _sb_attn(qkv, width, b, t, tq, 0, nb, 2 * nb)
    xp = _outproj(xp, [o], [wo])

    qkv_s, kv_s = _proj(xs, g, wq, [wkv])
    q_s = qkv_s[:, :width].astype(F32).reshape(db, n_q, width)
    page = cache_kv.shape[2]
    pool = _transposed_pool(cache_kv)
    n_pages = _pick_tile(page_table.shape[1], 4)
    rows = n_q * n_heads
    o_s = _decode_call(
        functools.partial(_sb_dec_kernel, n_pages=n_pages, n_heads=n_heads), "sb_decode", page_table, pool,
        layer, n_pages, [q_s, _transposed_new(kv_s, db, n_q, width)], [], [], n_q, width,
        [pltpu.VMEM((rows, width), BF16), pltpu.VMEM((rows, width), F32), pltpu.VMEM((rows, 1), F32),
         pltpu.VMEM((2, width, page), F32)])
    xs = _outproj(xs, [o_s.reshape(db * n_q, width)], [wo])
    return xp, xs, kv, kv_s


def kernel(x_prompt, x_sample, cache_fox_kv, cache_fox_logf, cache_diff_kv, cache_sb_kv, page_table, rel_bias_table, norm_g, final_norm_g, w_ffn_up, w_ffn_down, w_in_ab, b_forget, diff_lambda, diff_subln_g, w_out_ab, w_in_sb, w_out_sb):
    b, t, d = x_prompt.shape
    db, n_q, _ = x_sample.shape
    depth = norm_g.shape[0]
    page = cache_fox_kv.shape[2]
    assert page >= MAX_DISTANCE and page % LANES == 0 and n_q <= page
    tq = _pick_tile(t, 256)
    assert tq >= MAX_DISTANCE
    dims = (b, t, db, n_q)
    xp = x_prompt.reshape(b * t, d)
    xs = x_sample.reshape(db * n_q, d)
    bias_blk, dec_bias = _bias_tables(rel_bias_table, tq, n_q, page)

    p_fox_kv, p_fox_lf, p_diff_kv, p_sb_kv = [], [], [], []
    s_fox_kv, s_fox_lf, s_diff_kv, s_sb_kv = [], [], [], []
    for i in range(depth):
        l = i // 2
        last = i == depth - 1
        wu = w_ffn_up[i].astype(BF16)
        wd = w_ffn_down[i].astype(BF16)
        xp = _ffn(xp, norm_g[i, 0], wu[0], wd[0])
        xs = _ffn(xs, norm_g[i, 0], wu[0], wd[0])
        if i % 2 == 0:
            lam_init = 0.8 - 0.6 * math.exp(-0.3 * i)
            xp, xs, (kva, lf, kvb), (kva_s, lf_s, kvb_s) = _ab_layer(
                xp, xs, dims, page_table, cache_fox_kv, cache_fox_logf, cache_diff_kv, l, norm_g[i, 1],
                w_in_ab[l], b_forget[l], diff_lambda[l], diff_subln_g[l], w_out_ab[l], rel_bias_table,
                bias_blk, dec_bias, lam_init, tq)
            p_fox_kv.append(kva.reshape(b, t, 2, H_FOX, HEAD_DIM))
            p_fox_lf.append(lf.reshape(b, t, H_FOX))
            p_diff_kv.append(kvb.reshape(b, t, 2, H_DIFF, 2 * HEAD_DIM))
            s_fox_kv.append(kva_s.reshape(db, n_q, 2, H_FOX, HEAD_DIM))
            s_fox_lf.append(lf_s.reshape(db, n_q, H_FOX))
            s_diff_kv.append(kvb_s.reshape(db, n_q, 2, H_DIFF, 2 * HEAD_DIM))
        else:
            xp, xs, kv, kv_s = _sb_layer(xp, xs, dims, page_table, cache_sb_kv, l, norm_g[i, 1],
                                         w_in_sb[l], w_out_sb[l], tq)
            n_heads = kv.shape[1] // (2 * HEAD_DIM)
            p_sb_kv.append(kv.reshape(b, t, 2, n_heads, HEAD_DIM))
            s_sb_kv.append(kv_s.reshape(db, n_q, 2, n_heads, HEAD_DIM))
        fg = final_norm_g if last else None
        xp = _ffn(xp, norm_g[i, 2], wu[1], wd[1], final_g=fg)
        xs = _ffn(xs, norm_g[i, 2], wu[1], wd[1], final_g=fg)
    return (xp.reshape(b, t, d), xs.reshape(db, n_q, d),
            jnp.stack(p_fox_kv), jnp.stack(p_fox_lf), jnp.stack(p_diff_kv), jnp.stack(p_sb_kv),
            jnp.stack(s_fox_kv), jnp.stack(s_fox_lf), jnp.stack(s_diff_kv), jnp.stack(s_sb_kv))
```

```python
import functools
import math

import jax
import jax.numpy as jnp
from jax import lax
from jax.experimental import pallas as pl
from jax.experimental.pallas import tpu as pltpu

F32 = jnp.float32
BF16 = jnp.bfloat16

HEAD_DIM = 64
H_FOX = 8
H_DIFF = 4
FOX_W = H_FOX * HEAD_DIM
DIFF_W = H_DIFF * 2 * HEAD_DIM
N_BUCKETS = 32
MAX_EXACT = N_BUCKETS // 2
MAX_DISTANCE = 128
RMS_EPS = 1e-5
NEG = -1e30
Q_SCALE = HEAD_DIM ** -0.5

LANES = 128
MXU_DIM = 256
F_PAD = LANES
ATT_W = 4 * LANES
ONES_ROWS = 16
VMEM_LIMIT = 48 * 1024 * 1024

_NT = (((1,), (1,)), ((), ()))


def _cparams(*sem):
    return pltpu.CompilerParams(dimension_semantics=sem, vmem_limit_bytes=VMEM_LIMIT)


def _rmsnorm(x, g):
    return x * lax.rsqrt(jnp.mean(x * x, axis=-1, keepdims=True) + RMS_EPS) * g


def _log_sigmoid(x):
    return jnp.minimum(x, 0.0) - jnp.log(1.0 + jnp.exp(-jnp.abs(x)))


def _pick_tile(n, target):
    t = min(n, target)
    while n % t:
        t //= 2
    return t


def _ffn_kernel(x_ref, g_ref, wg_ref, wu_ref, wd_ref, *rest, final):
    if final:
        fg_ref, o_ref, h_ref, acc_ref = rest
    else:
        o_ref, h_ref, acc_ref = rest
    j = pl.program_id(1)

    @pl.when(j == 0)
    def _():
        h_ref[...] = _rmsnorm(x_ref[...], g_ref[...]).astype(BF16)
        acc_ref[...] = jnp.zeros_like(acc_ref)

    h = h_ref[...]
    gate = jnp.dot(h, wg_ref[...], preferred_element_type=F32)
    up = jnp.dot(h, wu_ref[...], preferred_element_type=F32)
    act = (gate * jax.nn.sigmoid(gate) * up).astype(BF16)
    acc_ref[...] += jnp.dot(act, wd_ref[...], preferred_element_type=F32)

    @pl.when(j == pl.num_programs(1) - 1)
    def _():
        y = x_ref[...] + 0.5 * acc_ref[...]
        if final:
            y = _rmsnorm(y, fg_ref[...])
        o_ref[...] = y


def _ffn(x, g, w_up, w_down, final_g=None, tm_target=512, tf_target=1408):
    m, d = x.shape
    d_ff = w_down.shape[0]
    tm = _pick_tile(m, tm_target)
    tf = _pick_tile(d_ff, tf_target)
    nf = d_ff // tf
    final = final_g is not None
    in_specs = [
        pl.BlockSpec((tm, d), lambda i, j: (i, 0)),
        pl.BlockSpec((1, d), lambda i, j: (0, 0)),
        pl.BlockSpec((d, tf), lambda i, j: (0, j)),
        pl.BlockSpec((d, tf), lambda i, j: (0, j + nf)),
        pl.BlockSpec((tf, d), lambda i, j: (j, 0)),
    ]
    args = [x, g.reshape(1, d), w_up, w_up, w_down]
    if final:
        in_specs.append(pl.BlockSpec((1, d), lambda i, j: (0, 0)))
        args.append(final_g.reshape(1, d))
    return pl.pallas_call(
        functools.partial(_ffn_kernel, final=final),
        grid=(m // tm, nf),
        in_specs=in_specs,
        out_specs=pl.BlockSpec((tm, d), lambda i, j: (i, 0)),
        out_shape=jax.ShapeDtypeStruct((m, d), F32),
        scratch_shapes=[pltpu.VMEM((tm, d), BF16), pltpu.VMEM((tm, d), F32)],
        compiler_params=_cparams("parallel", "arbitrary"),
        name="half_ffn",
    )(*args)


def _proj_kernel(x_ref, g_ref, wq_ref, *rest, n_kv, with_f):
    kv_w = rest[:n_kv]
    rest = rest[n_kv:]
    if with_f:
        wf_ref, bf_ref = rest[:2]
        rest = rest[2:]
    qkv_ref = rest[0]
    kv_o = rest[1:1 + n_kv]
    h = _rmsnorm(x_ref[...], g_ref[...]).astype(BF16)
    q = jnp.dot(h, wq_ref[...], preferred_element_type=F32) * Q_SCALE
    wq_cols = wq_ref.shape[1]
    qkv_ref[:, :wq_cols] = q.astype(BF16)
    off = wq_cols
    for w_ref, o_ref in zip(kv_w, kv_o):
        kv = jnp.dot(h, w_ref[...], preferred_element_type=F32)
        o_ref[...] = kv
        qkv_ref[:, off:off + kv.shape[1]] = kv.astype(BF16)
        off += kv.shape[1]
    if with_f:
        lf_ref = rest[1 + n_kv]
        f = jnp.dot(h, wf_ref[...], preferred_element_type=F32) + bf_ref[...]
        lf_ref[...] = _log_sigmoid(f)[:, :lf_ref.shape[1]]


def _proj(x, g, wq, kv_ws, wf=None, bf=None, n_f=0, tm_target=512):
    m, d = x.shape
    tm = _pick_tile(m, tm_target)
    with_f = wf is not None
    n_kv = len(kv_ws)
    tot = wq.shape[1] + sum(w.shape[1] for w in kv_ws)
    const = lambda i: (0, 0)
    row = lambda i: (i, 0)
    in_specs = [pl.BlockSpec((tm, d), row), pl.BlockSpec((1, d), const),
                pl.BlockSpec(wq.shape, const)]
    in_specs += [pl.BlockSpec(w.shape, const) for w in kv_ws]
    args = [x, g.reshape(1, d), wq, *kv_ws]
    out_shape = [jax.ShapeDtypeStruct((m, tot), BF16)]
    out_specs = [pl.BlockSpec((tm, tot), row)]
    for w in kv_ws:
        out_shape.append(jax.ShapeDtypeStruct((m, w.shape[1]), F32))
        out_specs.append(pl.BlockSpec((tm, w.shape[1]), row))
    if with_f:
        in_specs += [pl.BlockSpec(wf.shape, const), pl.BlockSpec(bf.shape, const)]
        args += [wf, bf]
        out_shape.append(jax.ShapeDtypeStruct((m, n_f), F32))
        out_specs.append(pl.BlockSpec((tm, n_f), row))
    return pl.pallas_call(
        functools.partial(_proj_kernel, n_kv=n_kv, with_f=with_f),
        grid=(m // tm,),
        in_specs=in_specs,
        out_specs=out_specs,
        out_shape=out_shape,
        compiler_params=_cparams("parallel"),
        name="in_proj",
    )(*args)


def _proj_t_kernel(x_ref, g_ref, wq_ref, wk_ref, wkvt_ref, *rest, ab):
    if ab:
        wvb_ref, wvbt_ref, wf_ref, bf_ref, q_o, k_o, vt_o, kvt_o, kvb_o, lf_o = rest
    else:
        q_o, k_o, vt_o, kvt_o = rest
    h = _rmsnorm(x_ref[...], g_ref[...]).astype(BF16)
    q_o[...] = (jnp.dot(h, wq_ref[...], preferred_element_type=F32) * Q_SCALE).astype(BF16)
    k = jnp.dot(h, wk_ref[...], preferred_element_type=F32)
    k_o[...] = k.astype(BF16)
    kvt = lax.dot_general(wkvt_ref[...], h, _NT, preferred_element_type=F32)
    kvt_o[0] = kvt
    w = kvt.shape[0] // 2
    vt_o[0, 0:w, :] = kvt[w:].astype(BF16)
    if ab:
        wb = wvb_ref.shape[1]
        kvb_o[:, 0:wb] = k[:, k.shape[1] - wb:]
        kvb_o[:, wb:] = jnp.dot(h, wvb_ref[...], preferred_element_type=F32)
        vt_o[0, w:, :] = lax.dot_general(wvbt_ref[...], h, _NT, preferred_element_type=F32).astype(BF16)
        f = jnp.dot(h, wf_ref[...], preferred_element_type=F32) + bf_ref[...]
        lf_o[...] = _log_sigmoid(f)[:, :lf_o.shape[1]]


def _proj_t(x, g, b, t, wq, wk, wkvt, ab_ws=None, n_f=0, tm_target=256):
    m, d = x.shape
    tm = _pick_tile(t, tm_target)
    nt = t // tm
    ab = ab_ws is not None
    const = lambda bi, ti: (0, 0)
    row = lambda bi, ti: (bi * nt + ti, 0)
    seq = lambda bi, ti: (bi, 0, ti)
    in_specs = [pl.BlockSpec((tm, d), row), pl.BlockSpec((1, d), const), pl.BlockSpec(wq.shape, const),
                pl.BlockSpec(wk.shape, const), pl.BlockSpec(wkvt.shape, const)]
    args = [x, g.reshape(1, d), wq, wk, wkvt]
    w2 = wkvt.shape[0]
    vt_rows = w2 // 2 + (ab_ws[1].shape[0] if ab else 0)
    out_shape = [jax.ShapeDtypeStruct((m, wq.shape[1]), BF16), jax.ShapeDtypeStruct((m, wk.shape[1]), BF16),
                 jax.ShapeDtypeStruct((b, vt_rows, t), BF16), jax.ShapeDtypeStruct((b, w2, t), F32)]
    out_specs = [pl.BlockSpec((tm, wq.shape[1]), row), pl.BlockSpec((tm, wk.shape[1]), row),
                 pl.BlockSpec((1, vt_rows, tm), seq), pl.BlockSpec((1, w2, tm), seq)]
    if ab:
        wvb, wvbt, wf, bf = ab_ws
        in_specs += [pl.BlockSpec(a.shape, const) for a in ab_ws]
        args += list(ab_ws)
        out_shape += [jax.ShapeDtypeStruct((m, 2 * wvb.shape[1]), F32), jax.ShapeDtypeStruct((m, n_f), F32)]
        out_specs += [pl.BlockSpec((tm, 2 * wvb.shape[1]), row), pl.BlockSpec((tm, n_f), row)]
    return pl.pallas_call(
        functools.partial(_proj_t_kernel, ab=ab),
        grid=(b, nt),
        in_specs=in_specs,
        out_specs=out_specs,
        out_shape=out_shape,
        compiler_params=_cparams("parallel", "parallel"),
        name="in_proj_t",
    )(*args)


def _outproj_kernel(x_ref, *rest, n_parts):
    o_refs = rest[:n_parts]
    w_refs = rest[n_parts:2 * n_parts]
    y_ref = rest[2 * n_parts]
    acc = x_ref[...]
    for o_ref, w_ref in zip(o_refs, w_refs):
        acc = acc + jnp.dot(o_ref[...].astype(BF16), w_ref[...], preferred_element_type=F32)
    y_ref[...] = acc


def _outproj(x, parts, ws, tm_target=512):
    m, d = x.shape
    tm = _pick_tile(m, tm_target)
    row = lambda i: (i, 0)
    const = lambda i: (0, 0)
    in_specs = [pl.BlockSpec((tm, d), row)]
    in_specs += [pl.BlockSpec((tm, p.shape[1]), row) for p in parts]
    in_specs += [pl.BlockSpec(w.shape, const) for w in ws]
    return pl.pallas_call(
        functools.partial(_outproj_kernel, n_parts=len(parts)),
        grid=(m // tm,),
        in_specs=in_specs,
        out_specs=pl.BlockSpec((tm, d), row),
        out_shape=jax.ShapeDtypeStruct((m, d), F32),
        compiler_params=_cparams("parallel"),
        name="out_proj",
    )(x, *parts, *ws)


def _cumsum_lanes_kernel(x_ref, o_ref):
    x = x_ref[0]
    t = x.shape[1]
    lane = lax.broadcasted_iota(jnp.int32, x.shape, 1)
    d = 1
    while d < t:
        x = x + jnp.where(lane >= d, pltpu.roll(x, d, axis=1), 0.0)
        d *= 2
    o_ref[0] = x


def _cumsum_lanes(x):
    b, h, t = x.shape
    spec = pl.BlockSpec((1, h, t), lambda i: (i, 0, 0))
    return pl.pallas_call(
        _cumsum_lanes_kernel, grid=(b,), in_specs=[spec], out_specs=spec,
        out_shape=jax.ShapeDtypeStruct(x.shape, F32),
        compiler_params=_cparams("parallel"), name="logf_cumsum",
    )(x)


def _bucket(n):
    nf = jnp.maximum(n, MAX_EXACT).astype(F32)
    large = MAX_EXACT + (jnp.log(nf / MAX_EXACT) / math.log(MAX_DISTANCE / MAX_EXACT)
                         * (N_BUCKETS - MAX_EXACT)).astype(jnp.int32)
    return jnp.where(n < MAX_EXACT, n, jnp.minimum(large, N_BUCKETS - 1))


def _bias_kernel(tbl_ref, blk_ref, dec_ref, *, tq, n_q, page):
    h = pl.program_id(0)
    r = lax.broadcasted_iota(jnp.int32, (tq, tq), 0)
    c = lax.broadcasted_iota(jnp.int32, (tq, tq), 1)
    for which, base in enumerate((0, tq)):
        bucket = _bucket(jnp.maximum(c - r + base, 0))
        val = jnp.zeros((tq, tq), F32)
        for b in range(N_BUCKETS):
            val = jnp.where(bucket == b, tbl_ref[b, h], val)
        blk_ref[0, which] = val
    n_rows = n_q * 2 * H_DIFF
    rr = lax.broadcasted_iota(jnp.int32, (n_rows, page), 0)
    jj = lax.broadcasted_iota(jnp.int32, (n_rows, page), 1)
    qi = rr % n_q
    hd = rr // (2 * n_q)
    dists = (page + qi - jj, jnp.maximum(qi - jj, 0), jnp.full((n_rows, page), MAX_DISTANCE, jnp.int32))
    for which, dist in enumerate(dists):
        bucket = _bucket(dist)
        val = jnp.zeros((n_rows, page), F32)
        for b in range(N_BUCKETS):
            tb = jnp.zeros((n_rows, page), F32)
            for hh in range(H_DIFF):
                tb = jnp.where(hd == hh, tbl_ref[b, hh], tb)
            val = jnp.where(bucket == b, tb, val)
        dec_ref[which] = val


def _bias_tables(table, tq, n_q, page):
    n_rows = n_q * 2 * H_DIFF
    return pl.pallas_call(
        functools.partial(_bias_kernel, tq=tq, n_q=n_q, page=page),
        grid=(H_DIFF,),
        in_specs=[pl.BlockSpec(memory_space=pltpu.SMEM)],
        out_specs=[pl.BlockSpec((1, 2, tq, tq), lambda h: (h, 0, 0, 0)),
                   pl.BlockSpec((3, n_rows, page), lambda h: (0, 0, 0))],
        out_shape=[jax.ShapeDtypeStruct((H_DIFF, 2, tq, tq), F32),
                   jax.ShapeDtypeStruct((3, n_rows, page), F32)],
        compiler_params=_cparams("arbitrary"),
        name="rel_bias_tables",
    )(table)


def _half_mask(tq, upper):
    lane = lax.broadcasted_iota(jnp.int32, (tq, LANES), 1)
    return lane >= HEAD_DIM if upper else lane < HEAD_DIM


def _tile_iota(tq):
    row = lax.broadcasted_iota(jnp.int32, (tq, 2 * tq), 0)
    col = lax.broadcasted_iota(jnp.int32, (tq, 2 * tq), 1)
    return row, jnp.where(col >= tq, col - tq, col)


def _split_halves(q):
    zero = jnp.zeros_like(q)
    return jnp.concatenate([jnp.where(_half_mask(q.shape[0], False), q, zero),
                            jnp.where(_half_mask(q.shape[0], True), q, zero)], axis=0)


def _with_ones_rows(vt):
    return jnp.concatenate([vt, jnp.ones((ONES_ROWS, vt.shape[1]), vt.dtype)], axis=0)


def _softmax_chain_step(st, vt, m_old, acc):
    m_new = jnp.maximum(m_old, jnp.max(st, axis=0, keepdims=True))
    alpha = jnp.exp(m_old - m_new)
    pt = jnp.exp(st - m_new).astype(BF16)
    return m_new, alpha * acc + jnp.dot(vt, pt, preferred_element_type=F32)


def _run_tiles(qi, blk, state, diag_first):
    if diag_first:
        state = blk(qi, state, True)
        return lax.fori_loop(0, qi, lambda t, s: blk(qi - 1 - t, s, False), state)
    state = lax.fori_loop(0, qi, lambda kj, s: blk(kj, s, False), state)
    return blk(qi, state, True)


def _init_softmax_state(m_sc, l_sc, acc_sc):
    m_sc[...] = jnp.full_like(m_sc, NEG)
    l_sc[...] = jnp.zeros_like(l_sc)
    acc_sc[...] = jnp.zeros_like(acc_sc)


def _store_pair(o_ref, p, top, bottom):
    pair_t = jnp.concatenate([top, bottom], axis=0)
    o_ref[:, p * LANES:(p + 1) * LANES] = pair_t.T.astype(BF16)


def _fox_attn_kernel(q_ref, k_ref, vt_ref, ft_ref, fc_ref, o_ref, *, tq):
    qi = pl.program_id(2)
    n_heads = ATT_W // HEAD_DIM
    row, col = _tile_iota(tq)
    first = lax.broadcasted_iota(jnp.int32, (tq, 2 * tq), 1) < tq
    qoff = pl.multiple_of(qi * tq, tq)

    def blk(kj, state, masked):
        koff = pl.multiple_of(kj * tq, tq)
        new = []
        for p in range(n_heads // 2):
            pr = slice(p * LANES, (p + 1) * LANES)
            k = k_ref[pl.ds(koff, tq), pr]
            st = lax.dot_general(k, _split_halves(q_ref[:, pr]), _NT, preferred_element_type=F32)
            fq = jnp.concatenate([ft_ref[0, h:h + 1, pl.ds(qoff, tq)] for h in (2 * p, 2 * p + 1)], axis=1)
            fk = jnp.where(first, fc_ref[0, pl.ds(koff, tq), 2 * p:2 * p + 1],
                           fc_ref[0, pl.ds(koff, tq), 2 * p + 1:2 * p + 2])
            st = st + (fq - fk)
            if masked:
                st = jnp.where(row <= col, st, NEG)
            vt = _with_ones_rows(vt_ref[pr, pl.ds(koff, tq)])
            new.extend(_softmax_chain_step(st, vt, state[2 * p], state[2 * p + 1]))
        return tuple(new)

    init = (jnp.full((1, 2 * tq), NEG, F32), jnp.zeros((LANES + ONES_ROWS, 2 * tq), F32)) * (n_heads // 2)
    state = _run_tiles(qi, blk, init, False)
    for p in range(n_heads // 2):
        a = state[2 * p + 1]
        _store_pair(o_ref, p, a[:HEAD_DIM, :tq] / a[LANES:LANES + 1, :tq],
                    a[HEAD_DIM:LANES, tq:] / a[LANES:LANES + 1, tq:])


def _diff_lambda(lp, lam_init):
    return (jnp.exp(jnp.sum(lp[0:1] * lp[1:2], axis=-1, keepdims=True))
            - jnp.exp(jnp.sum(lp[2:3] * lp[3:4], axis=-1, keepdims=True)) + lam_init)


def _diff_attn_kernel(tbl_ref, q_ref, k_ref, vt_ref, bias_ref, lam_ref, g_ref, o_ref, *, tq, lam_init):
    qi = pl.program_id(2)
    row, col = _tile_iota(tq)
    ve = 2 * HEAD_DIM

    def blk(kj, state, masked):
        koff = pl.multiple_of(kj * tq, tq)
        new = []
        for h in range(H_DIFF):
            hs = slice(h * LANES, (h + 1) * LANES)
            q = q_ref[:, hs]
            k = k_ref[pl.ds(koff, tq), hs]
            vt = _with_ones_rows(vt_ref[h * ve:(h + 1) * ve, pl.ds(koff, tq)])
            if masked:
                bias = bias_ref[h, 0]
            else:
                bias = jnp.where(kj == qi - 1, bias_ref[h, 1], tbl_ref[N_BUCKETS - 1, h])
            st = lax.dot_general(k, _split_halves(q), _NT, preferred_element_type=F32)
            st = st + jnp.concatenate([bias, bias], axis=1)
            if masked:
                st = jnp.where(row <= col, st, NEG)
            new.extend(_softmax_chain_step(st, vt, state[2 * h], state[2 * h + 1]))
        return tuple(new)

    init = (jnp.full((1, 2 * tq), NEG, F32), jnp.zeros((ve + ONES_ROWS, 2 * tq), F32)) * H_DIFF
    state = _run_tiles(qi, blk, init, False)
    lam = _diff_lambda(lam_ref[...], lam_init)
    for h in range(H_DIFF):
        a = state[2 * h + 1]
        ot = a[:ve, :tq] / a[ve:ve + 1, :tq] - lam * (a[:ve, tq:] / a[ve:ve + 1, tq:])
        ot = ot * lax.rsqrt(jnp.mean(ot * ot, axis=0, keepdims=True) + RMS_EPS)
        o = ot.T * g_ref[...] * (1.0 - lam_init)
        o_ref[:, h * LANES:(h + 1) * LANES] = o.astype(BF16)


def _strict_upper_sum_matrix(n):
    j = lax.broadcasted_iota(jnp.int32, (n, n), 0)
    s = lax.broadcasted_iota(jnp.int32, (n, n), 1)
    return j > s


def _suffix_sums_2pass(x, mtri):
    hi = x.astype(BF16)
    lo = (x - hi.astype(F32)).astype(BF16)
    return (jnp.dot(hi, mtri, preferred_element_type=F32) + jnp.dot(lo, mtri, preferred_element_type=F32))


def _sb_attn_kernel(q_ref, k_ref, vt_ref, o_ref, *, tq):
    qi = pl.program_id(2)
    n_heads = ATT_W // HEAD_DIM
    row, col = _tile_iota(tq)
    sub = min(tq, MXU_DIM)

    def blk(kj, state, masked):
        koff = pl.multiple_of(kj * tq, tq)
        mtri_t = jnp.where(lax.broadcasted_iota(jnp.int32, (sub, sub), 1)
                           > lax.broadcasted_iota(jnp.int32, (sub, sub), 0), 1.0, 0.0).astype(BF16)
        new = []
        for p in range(n_heads // 2):
            pr = slice(p * LANES, (p + 1) * LANES)
            k = k_ref[pl.ds(koff, tq), pr]
            zt = lax.dot_general(k, _split_halves(q_ref[:, pr]), _NT, preferred_element_type=F32)
            lsz = _log_sigmoid(zt)
            lrest = lsz - zt
            if masked:
                lrest = jnp.where(row < col, lrest, 0.0)
            hi = lrest.astype(BF16)
            lo = (lrest - hi.astype(F32)).astype(BF16)
            hl = jnp.concatenate([hi, lo], axis=1)
            later = jnp.zeros((1, 2 * tq), F32)
            parts = []
            for sb in reversed(range(tq // sub)):
                rs = slice(sb * sub, (sb + 1) * sub)
                c2 = jnp.dot(mtri_t, hl[rs], preferred_element_type=F32)
                c = c2[:, :2 * tq] + c2[:, 2 * tq:]
                parts.append(c + later)
                later = later + c[0:1, :] + lrest[sb * sub:sb * sub + 1, :]
            cs = jnp.concatenate(parts[::-1], axis=0)
            run, acc = state[2 * p], state[2 * p + 1]
            a = jnp.exp(lsz + cs + run)
            if masked:
                a = jnp.where(row < col, a, 0.0)
            vt = vt_ref[pr, pl.ds(koff, tq)]
            new.append(run + cs[0:1, :] + lrest[0:1, :])
            new.append(acc + jnp.dot(vt, a.astype(BF16), preferred_element_type=F32))
        return tuple(new)

    init = (jnp.zeros((1, 2 * tq), F32), jnp.zeros((LANES, 2 * tq), F32)) * (n_heads // 2)
    state = _run_tiles(qi, blk, init, True)
    for p in range(n_heads // 2):
        acc = state[2 * p + 1]
        _store_pair(o_ref, p, acc[:HEAD_DIM, :tq], acc[HEAD_DIM:, tq:])


def _attn_specs(t, tq, blk0):
    nq = t // tq
    q_spec = pl.BlockSpec((tq, ATT_W), lambda bi, g, qi: (bi * nq + qi, blk0 + g))
    k_spec = pl.BlockSpec((t, ATT_W), lambda bi, g, qi: (bi, blk0 + g))
    vt_spec = pl.BlockSpec((None, ATT_W, t), lambda bi, g, qi: (bi, blk0 + g, 0))
    o_spec = pl.BlockSpec((tq, ATT_W), lambda bi, g, qi: (bi * nq + qi, g))
    return nq, q_spec, k_spec, vt_spec, o_spec


_ATTN_SEM = ("parallel", "parallel", "arbitrary")


def _fox_attn(q, k, vt, ft, fc, b, t, tq, blk0):
    nq, q_spec, k_spec, vt_spec, o_spec = _attn_specs(t, tq, blk0)
    n_heads = ATT_W // HEAD_DIM
    return pl.pallas_call(
        functools.partial(_fox_attn_kernel, tq=tq),
        grid=(b, FOX_W // ATT_W, nq),
        in_specs=[q_spec, k_spec, vt_spec,
                  pl.BlockSpec((1, n_heads, t), lambda bi, g, qi: (bi, g, 0)),
                  pl.BlockSpec((1, t, n_heads), lambda bi, g, qi: (bi, 0, g))],
        out_specs=o_spec,
        out_shape=jax.ShapeDtypeStruct((b * t, FOX_W), BF16),
        compiler_params=_cparams(*_ATTN_SEM),
        name="fox_attn",
    )(q, k, vt, ft, fc)


def _diff_attn(q, k, vt, table, bias_blk, lam_p, subln_g, lam_init, b, t, tq, blk0):
    nq, q_spec, k_spec, vt_spec, o_spec = _attn_specs(t, tq, blk0)
    return pl.pallas_call(
        functools.partial(_diff_attn_kernel, tq=tq, lam_init=lam_init),
        grid=(b, DIFF_W // ATT_W, nq),
        in_specs=[pl.BlockSpec(memory_space=pltpu.SMEM), q_spec, k_spec, vt_spec,
                  pl.BlockSpec(bias_blk.shape, lambda bi, g, qi: (0, 0, 0, 0)),
                  pl.BlockSpec(lam_p.shape, lambda bi, g, qi: (0, 0)),
                  pl.BlockSpec((1, LANES), lambda bi, g, qi: (0, 0))],
        out_specs=o_spec,
        out_shape=jax.ShapeDtypeStruct((b * t, DIFF_W), BF16),
        compiler_params=_cparams(*_ATTN_SEM),
        name="diff_attn",
    )(table, q, k, vt, bias_blk, lam_p, subln_g.reshape(1, LANES))


def _sb_attn(q, k, vt, width, b, t, tq):
    nq, q_spec, k_spec, vt_spec, o_spec = _attn_specs(t, tq, 0)
    return pl.pallas_call(
        functools.partial(_sb_attn_kernel, tq=tq),
        grid=(b, width // ATT_W, nq),
        in_specs=[q_spec, k_spec, vt_spec],
        out_specs=o_spec,
        out_shape=jax.ShapeDtypeStruct((b * t, width), BF16),
        compiler_params=_cparams(*_ATTN_SEM),
        name="sb_attn",
    )(q, k, vt)


def _head_mask(n_heads, width, group):
    hrow = lax.broadcasted_iota(jnp.int32, (n_heads, width), 0)
    hcol = lax.broadcasted_iota(jnp.int32, (n_heads, width), 1) // group
    return hrow == hcol


def _build_wq(q, n_heads):
    n_q, w = q.shape
    mask = _head_mask(n_heads, w, HEAD_DIM)
    parts = [jnp.where(mask, jnp.broadcast_to(q[i:i + 1, :], (n_heads, w)), 0.0) for i in range(n_q)]
    return jnp.concatenate(parts, axis=0).astype(BF16)


def _query_of_row(n_rows, n_heads, width):
    return lax.broadcasted_iota(jnp.int32, (n_rows, width), 0) // n_heads


def _fill_new_page(new_sc, kvnew_ref):
    new_sc[...] = jnp.zeros_like(new_sc)
    new_sc[:, :, 0:kvnew_ref.shape[3]] = kvnew_ref[0]


def _dec_softmax_update(s, vts, m_sc, l_sc, acc_sc):
    page = vts[0].shape[1]
    m_old = m_sc[...]
    m_new = jnp.maximum(m_old, jnp.max(s, axis=-1, keepdims=True))
    alpha = jnp.exp(m_old - m_new)
    p = jnp.exp(s - m_new)
    l_sc[...] = alpha * l_sc[...] + jnp.sum(p, axis=-1, keepdims=True)
    pb = p.astype(BF16)
    acc = alpha * acc_sc[...]
    for g, vt in enumerate(vts):
        acc = acc + lax.dot_general(pb[:, g * page:(g + 1) * page], vt, _NT, preferred_element_type=F32)
    acc_sc[...] = acc
    m_sc[...] = m_new


def _gather_heads(acc, n_q, n_heads, mask):
    rows = [jnp.sum(jnp.where(mask, acc[i * n_heads:(i + 1) * n_heads], 0.0), axis=0, keepdims=True)
            for i in range(n_q)]
    return jnp.concatenate(rows, axis=0)


def _fox_dec_kernel(pt_ref, q_ref, kvnew_ref, lfn_ref, *refs, n_pages):
    del pt_ref
    pages = refs[:n_pages]
    lfs = refs[n_pages:2 * n_pages]
    o_ref = refs[2 * n_pages]
    wq_sc, m_sc, l_sc, acc_sc, car_sc, fq_sc, new_sc = refs[2 * n_pages + 1:]
    p = pl.program_id(1)
    n_q, w = q_ref.shape[1], q_ref.shape[2]
    n_rows = n_q * H_FOX
    page = pages[0].shape[2]

    @pl.when(p == 0)
    def _():
        wq = _build_wq(q_ref[0], H_FOX)
        wq_sc[...] = wq
        lfn = lfn_ref[0]
        run = jnp.zeros((H_FOX, 1), F32)
        cums = []
        for i in range(n_q):
            run = run + lfn[:, i:i + 1]
            cums.append(run)
        fq = jnp.concatenate(cums, axis=0)
        fq_sc[...] = fq
        lane = lax.broadcasted_iota(jnp.int32, (H_FOX, page), 1)
        fnew = jnp.zeros((H_FOX, page), F32)
        for j in range(n_q):
            fnew = jnp.where(lane == j, cums[j], fnew)
        fnew = jnp.concatenate([fnew] * n_q, axis=0)
        _fill_new_page(new_sc, kvnew_ref)
        s = jnp.dot(wq, new_sc[0].astype(BF16), preferred_element_type=F32) + (fq - fnew)
        jj = lax.broadcasted_iota(jnp.int32, (n_rows, page), 1)
        s = jnp.where(jj <= _query_of_row(n_rows, H_FOX, page), s, NEG)
        _init_softmax_state(m_sc, l_sc, acc_sc)
        car_sc[...] = jnp.zeros_like(car_sc)
        _dec_softmax_update(s, [new_sc[1].astype(BF16)], m_sc, l_sc, acc_sc)

    wq = wq_sc[...]
    mtri = jnp.where(_strict_upper_sum_matrix(page), 1.0, 0.0).astype(F32)
    lfx = jnp.concatenate([lf[...] for lf in lfs], axis=0)
    sfx = jnp.dot(lfx, mtri, preferred_element_type=F32, precision=lax.Precision.HIGHEST)
    carry = car_sc[...]
    s_parts = []
    for g in range(n_pages):
        sl = slice(g * H_FOX, (g + 1) * H_FOX)
        suf = sfx[sl] + carry
        carry = carry + sfx[sl, 0:1] + lfx[sl, 0:1]
        s = jnp.dot(wq, pages[g][0].astype(BF16), preferred_element_type=F32)
        s_parts.append(s + jnp.concatenate([suf] * n_q, axis=0))
    car_sc[...] = carry
    s = jnp.concatenate(s_parts, axis=1) + fq_sc[...]
    _dec_softmax_update(s, [pg[1].astype(BF16) for pg in pages], m_sc, l_sc, acc_sc)

    @pl.when(p == pl.num_programs(1) - 1)
    def _():
        accn = acc_sc[...] / l_sc[...]
        o_ref[0] = _gather_heads(accn, n_q, H_FOX, _head_mask(H_FOX, w, HEAD_DIM))


def _diff_dec_kernel(pt_ref, q_ref, kvnew_ref, dec_ref, lam_ref, g_ref, *refs, n_pages, lam_init):
    del pt_ref
    pages = refs[:n_pages]
    o_ref = refs[n_pages]
    m_sc, l_sc, acc_sc, new_sc = refs[n_pages + 1:]
    p = pl.program_id(1)
    n_q = o_ref.shape[1]
    rph = 2 * n_q
    n_rows = H_DIFF * rph
    stride = 2 * H_DIFF
    page = pages[0].shape[0] // stride

    def scores(ref):
        parts = []
        for h in range(H_DIFF):
            kh = ref[pl.ds(h, page, stride=stride), :].astype(BF16)
            qh = q_ref[0, h * rph:(h + 1) * rph, :].astype(BF16)
            parts.append(lax.dot_general(qh, kh, _NT, preferred_element_type=F32))
        return jnp.concatenate(parts, axis=0)

    def values(ref):
        return [ref[pl.ds(H_DIFF + h, page, stride=stride), :].astype(BF16) for h in range(H_DIFF)]

    def update(s, vals):
        m_old = m_sc[...]
        m_new = jnp.maximum(m_old, jnp.max(s, axis=-1, keepdims=True))
        alpha = jnp.exp(m_old - m_new)
        pr = jnp.exp(s - m_new)
        l_sc[...] = alpha * l_sc[...] + jnp.sum(pr, axis=-1, keepdims=True)
        acc = alpha * acc_sc[...]
        rows = []
        for h in range(H_DIFF):
            rs = slice(h * rph, (h + 1) * rph)
            a = acc[rs]
            for g, vs in enumerate(vals):
                a = a + jnp.dot(pr[rs, g * page:(g + 1) * page].astype(BF16), vs[h],
                                preferred_element_type=F32)
            rows.append(a)
        acc_sc[...] = jnp.concatenate(rows, axis=0)
        m_sc[...] = m_new

    @pl.when(p == 0)
    def _():
        new_sc[...] = jnp.zeros_like(new_sc)
        new_sc[0:kvnew_ref.shape[1], :] = kvnew_ref[0]
        s = scores(new_sc) + dec_ref[1]
        jj = lax.broadcasted_iota(jnp.int32, (n_rows, page), 1)
        ii = lax.broadcasted_iota(jnp.int32, (n_rows, page), 0) % n_q
        s = jnp.where(jj <= ii, s, NEG)
        _init_softmax_state(m_sc, l_sc, acc_sc)
        update(s, [values(new_sc)])

    far = dec_ref[2]
    s_parts = []
    for g in range(n_pages):
        bias = jnp.where(p == 0, dec_ref[0], far) if g == 0 else far
        s_parts.append(scores(pages[g]) + bias)
    update(jnp.concatenate(s_parts, axis=1), [values(pg) for pg in pages])

    @pl.when(p == pl.num_programs(1) - 1)
    def _():
        lam = _diff_lambda(lam_ref[...], lam_init)
        accn = acc_sc[...] / l_sc[...]
        outs = []
        for h in range(H_DIFF):
            blk = accn[h * rph:(h + 1) * rph]
            oh = blk[0:n_q] - lam * blk[n_q:rph]
            outs.append(_rmsnorm(oh, g_ref[...]) * (1.0 - lam_init))
        o_ref[0] = jnp.concatenate(outs, axis=1)


def _sb_dec_kernel(pt_ref, q_ref, kvnew_ref, *refs, n_pages, n_heads):
    del pt_ref
    pages = refs[:n_pages]
    o_ref = refs[n_pages]
    wq_sc, acc_sc, run_sc, new_sc = refs[n_pages + 1:]
    p = pl.program_id(1)
    n_q, w = q_ref.shape[1], q_ref.shape[2]
    n_rows = n_q * n_heads
    page = pages[0].shape[2]

    def absorb(wq, kts, vts, strict):
        mtri = jnp.where(_strict_upper_sum_matrix(page), 1.0, 0.0).astype(BF16)
        lszs, lrests = [], []
        for kt in kts:
            z = jnp.dot(wq, kt, preferred_element_type=F32)
            lsz = _log_sigmoid(z)
            lrest = lsz - z
            if strict is not None:
                lrest = jnp.where(strict, lrest, 0.0)
            lszs.append(lsz)
            lrests.append(lrest)
        cs_all = _suffix_sums_2pass(jnp.concatenate(lrests, axis=0), mtri)
        run = run_sc[...]
        acc = acc_sc[...]
        for g, vt in enumerate(vts):
            cs = cs_all[g * n_rows:(g + 1) * n_rows]
            a = jnp.exp(lszs[g] + cs + run)
            if strict is not None:
                a = jnp.where(strict, a, 0.0)
            acc = acc + lax.dot_general(a.astype(BF16), vt, _NT, preferred_element_type=F32)
            run = run + cs[:, 0:1] + lrests[g][:, 0:1]
        run_sc[...] = run
        acc_sc[...] = acc

    @pl.when(p == 0)
    def _():
        wq = _build_wq(q_ref[0], n_heads)
        wq_sc[...] = wq
        _fill_new_page(new_sc, kvnew_ref)
        acc_sc[...] = jnp.zeros_like(acc_sc)
        run_sc[...] = jnp.zeros_like(run_sc)
        jj = lax.broadcasted_iota(jnp.int32, (n_rows, page), 1)
        strict = jj < _query_of_row(n_rows, n_heads, page)
        absorb(wq, [new_sc[0].astype(BF16)], [new_sc[1].astype(BF16)], strict)

    absorb(wq_sc[...], [pg[0].astype(BF16) for pg in pages], [pg[1].astype(BF16) for pg in pages], None)

    @pl.when(p == pl.num_programs(1) - 1)
    def _():
        o_ref[0] = _gather_heads(acc_sc[...], n_q, n_heads, _head_mask(n_heads, w, HEAD_DIM))


def _page_specs(pool, layer, n_steps, n_pages):
    total = n_steps * n_pages
    tail = (0,) * (pool.ndim - 2)
    specs = []
    for g in range(n_pages):
        specs.append(pl.BlockSpec(
            (None, None) + pool.shape[2:],
            lambda b, p, pt, g=g: (layer, pt[b, total - 1 - (p * n_pages + g)]) + tail))
    return specs


def _per_batch_spec(shape):
    return pl.BlockSpec((1,) + shape[1:], lambda b, p, pt: (b,) + (0,) * (len(shape) - 1))


def _const_spec(shape):
    return pl.BlockSpec(shape, lambda b, p, pt: (0,) * len(shape))


def _decode_call(kernel_fn, name, page_table, pool, layer, n_pages, batch_args, const_args, extra_pools,
                 n_q, out_w, scratch):
    db, n_tbl = page_table.shape
    n_steps = n_tbl // n_pages
    in_specs = [_per_batch_spec(a.shape) for a in batch_args] + [_const_spec(a.shape) for a in const_args]
    args = list(batch_args) + list(const_args)
    for pl_arr in (pool,) + tuple(extra_pools):
        in_specs += _page_specs(pl_arr, layer, n_steps, n_pages)
        args += [pl_arr] * n_pages
    return pl.pallas_call(
        kernel_fn,
        grid_spec=pltpu.PrefetchScalarGridSpec(
            num_scalar_prefetch=1, grid=(db, n_steps), in_specs=in_specs,
            out_specs=pl.BlockSpec((1, n_q, out_w), lambda b, p, pt: (b, 0, 0)),
            scratch_shapes=scratch),
        out_shape=jax.ShapeDtypeStruct((db, n_q, out_w), F32),
        compiler_params=_cparams("parallel", "arbitrary"),
        name=name,
    )(page_table, *args)


def _transposed_pool(cache):
    l, n_pool, page, two, h, dd = cache.shape
    return cache.transpose(0, 1, 3, 4, 5, 2).reshape(l, n_pool, two, h * dd, page)


def _transposed_new(kv_s, db, n_q, width):
    return kv_s.reshape(db, n_q, 2, width).transpose(0, 2, 3, 1)


def _ab_layer(xp, xs, dims, page_table, cache_kv_a, cache_lf, cache_kv_b, layer, g, w_in, b_f, lam_p,
              subln_g, w_out, table, bias_blk, dec_bias, lam_init, tq):
    b, t, db, n_q = dims
    splits = [0, FOX_W, 2 * FOX_W, 3 * FOX_W, 3 * FOX_W + H_FOX, 3 * FOX_W + H_FOX + DIFF_W,
              3 * FOX_W + H_FOX + 2 * DIFF_W, 3 * FOX_W + H_FOX + 3 * DIFF_W]
    q_a, k_a, v_a, f_a, q_b, k_b, v_b = [w_in[:, splits[i]:splits[i + 1]] for i in range(7)]
    wq = jnp.concatenate([q_a, q_b], axis=1).astype(BF16)
    wkva = jnp.concatenate([k_a, v_a], axis=1).astype(BF16)
    wkvb = jnp.concatenate([k_b, v_b], axis=1).astype(BF16)
    wf = jnp.pad(f_a, ((0, 0), (0, F_PAD - H_FOX))).astype(BF16)
    bf = jnp.pad(b_f, (0, F_PAD - H_FOX)).reshape(1, F_PAD)
    wo_a = w_out[:FOX_W].astype(BF16)
    wo_b = w_out[FOX_W:].astype(BF16)
    wk = jnp.concatenate([k_a, k_b], axis=1).astype(BF16)
    q, k, vt, kvat, kvb, lf = _proj_t(xp, g, b, t, wq, wk, wkva.T,
                                      (v_b.astype(BF16), v_b.T.astype(BF16), wf, bf), H_FOX)
    ft = _cumsum_lanes(lf.reshape(b, t, H_FOX).transpose(0, 2, 1))
    fc = ft.transpose(0, 2, 1)
    o_a = _fox_attn(q, k, vt, ft, fc, b, t, tq, 0)
    o_b = _diff_attn(q, k, vt, table, bias_blk, lam_p, subln_g, lam_init, b, t, tq, FOX_W // ATT_W)
    xp = _outproj(xp, [o_a, o_b], [wo_a, wo_b])
    kva = kvat.reshape(b, 2, H_FOX, HEAD_DIM, t).transpose(0, 4, 1, 2, 3)

    qkv_s, kva_s, kvb_s, lf_s = _proj(xs, g, wq, [wkva, wkvb], wf, bf, H_FOX)
    q_s = qkv_s[:, :FOX_W + DIFF_W].astype(F32).reshape(db, n_q, FOX_W + DIFF_W)
    lfn = lf_s.reshape(db, n_q, H_FOX).transpose(0, 2, 1)
    n_pool, page = cache_kv_a.shape[1], cache_kv_a.shape[2]
    pool_a = _transposed_pool(cache_kv_a)
    pool_lf = cache_lf.transpose(0, 1, 3, 2)
    pool_b = cache_kv_b.reshape(cache_kv_b.shape[0], n_pool, page * 2 * H_DIFF, 2 * HEAD_DIM)
    n_pages = _pick_tile(page_table.shape[1], 8)
    rows_a = n_q * H_FOX
    o_a_s = _decode_call(
        functools.partial(_fox_dec_kernel, n_pages=n_pages), "fox_decode", page_table, pool_a, layer, n_pages,
        [q_s[:, :, :FOX_W], _transposed_new(kva_s, db, n_q, FOX_W), lfn], [], [pool_lf], n_q, FOX_W,
        [pltpu.VMEM((rows_a, FOX_W), BF16), pltpu.VMEM((rows_a, 1), F32), pltpu.VMEM((rows_a, 1), F32),
         pltpu.VMEM((rows_a, FOX_W), F32), pltpu.VMEM((H_FOX, 1), F32), pltpu.VMEM((rows_a, 1), F32),
         pltpu.VMEM((2, FOX_W, page), F32)])
    qb = q_s[:, :, FOX_W:].reshape(db, n_q, H_DIFF, 2, HEAD_DIM).transpose(0, 2, 3, 1, 4)
    q_rows = (qb[:, :, :, :, None, :] * jnp.eye(2, dtype=F32)[None, None, :, None, :, None]).reshape(
        db, H_DIFF * 2 * n_q, 2 * HEAD_DIM)
    rows_b = n_q * 2 * H_DIFF
    o_b_s = _decode_call(
        functools.partial(_diff_dec_kernel, n_pages=n_pages, lam_init=lam_init), "diff_decode", page_table,
        pool_b, layer, n_pages,
        [q_rows, kvb_s.reshape(db, n_q * 2 * H_DIFF, 2 * HEAD_DIM)],
        [dec_bias, lam_p, subln_g.reshape(1, LANES)], [], n_q, DIFF_W,
        [pltpu.VMEM((rows_b, 1), F32), pltpu.VMEM((rows_b, 1), F32), pltpu.VMEM((rows_b, 2 * HEAD_DIM), F32),
         pltpu.VMEM((page * 2 * H_DIFF, 2 * HEAD_DIM), F32)])
    xs = _outproj(xs, [o_a_s.reshape(db * n_q, FOX_W), o_b_s.reshape(db * n_q, DIFF_W)], [wo_a, wo_b])
    return xp, xs, (kva, lf, kvb), (kva_s, lf_s, kvb_s)


def _sb_layer(xp, xs, dims, page_table, cache_kv, layer, g, w_in, w_out, tq):
    b, t, db, n_q = dims
    width = w_in.shape[1] // 3
    n_heads = width // HEAD_DIM
    wq = w_in[:, :width].astype(BF16)
    wkv = w_in[:, width:].astype(BF16)
    wo = w_out.astype(BF16)

    q, k, vt, kvt = _proj_t(xp, g, b, t, wq, wkv[:, :width], wkv.T)
    o = _sb_attn(q, k, vt, width, b, t, tq)
    xp = _outproj(xp, [o], [wo])
    kv = kvt.reshape(b, 2, n_heads, HEAD_DIM, t).transpose(0, 4, 1, 2, 3)

    qkv_s, kv_s = _proj(xs, g, wq, [wkv])
    q_s = qkv_s[:, :width].astype(F32).reshape(db, n_q, width)
    page = cache_kv.shape[2]
    pool = _transposed_pool(cache_kv)
    n_pages = _pick_tile(page_table.shape[1], 4)
    rows = n_q * n_heads
    o_s = _decode_call(
        functools.partial(_sb_dec_kernel, n_pages=n_pages, n_heads=n_heads), "sb_decode", page_table, pool,
        layer, n_pages, [q_s, _transposed_new(kv_s, db, n_q, width)], [], [], n_q, width,
        [pltpu.VMEM((rows, width), BF16), pltpu.VMEM((rows, width), F32), pltpu.VMEM((rows, 1), F32),
         pltpu.VMEM((2, width, page), F32)])
    xs = _outproj(xs, [o_s.reshape(db * n_q, width)], [wo])
    return xp, xs, kv, kv_s


def kernel(x_prompt, x_sample, cache_fox_kv, cache_fox_logf, cache_diff_kv, cache_sb_kv, page_table, rel_bias_table, norm_g, final_norm_g, w_ffn_up, w_ffn_down, w_in_ab, b_forget, diff_lambda, diff_subln_g, w_out_ab, w_in_sb, w_out_sb):
    b, t, d = x_prompt.shape
    db, n_q, _ = x_sample.shape
    depth = norm_g.shape[0]
    page = cache_fox_kv.shape[2]
    assert page >= MAX_DISTANCE and page % LANES == 0 and n_q <= page
    tq = _pick_tile(t, 512)
    assert tq >= MAX_DISTANCE
    dims = (b, t, db, n_q)
    xp = x_prompt.reshape(b * t, d)
    xs = x_sample.reshape(db * n_q, d)
    bias_blk, dec_bias = _bias_tables(rel_bias_table, tq, n_q, page)

    p_fox_kv, p_fox_lf, p_diff_kv, p_sb_kv = [], [], [], []
    s_fox_kv, s_fox_lf, s_diff_kv, s_sb_kv = [], [], [], []
    for i in range(depth):
        l = i // 2
        last = i == depth - 1
        wu = w_ffn_up[i].astype(BF16)
        wd = w_ffn_down[i].astype(BF16)
        xp = _ffn(xp, norm_g[i, 0], wu[0], wd[0])
        xs = _ffn(xs, norm_g[i, 0], wu[0], wd[0])
        if i % 2 == 0:
            lam_init = 0.8 - 0.6 * math.exp(-0.3 * i)
            xp, xs, (kva, lf, kvb), (kva_s, lf_s, kvb_s) = _ab_layer(
                xp, xs, dims, page_table, cache_fox_kv, cache_fox_logf, cache_diff_kv, l, norm_g[i, 1],
                w_in_ab[l], b_forget[l], diff_lambda[l], diff_subln_g[l], w_out_ab[l], rel_bias_table,
                bias_blk, dec_bias, lam_init, tq)
            p_fox_kv.append(kva.reshape(b, t, 2, H_FOX, HEAD_DIM))
            p_fox_lf.append(lf.reshape(b, t, H_FOX))
            p_diff_kv.append(kvb.reshape(b, t, 2, H_DIFF, 2 * HEAD_DIM))
            s_fox_kv.append(kva_s.reshape(db, n_q, 2, H_FOX, HEAD_DIM))
            s_fox_lf.append(lf_s.reshape(db, n_q, H_FOX))
            s_diff_kv.append(kvb_s.reshape(db, n_q, 2, H_DIFF, 2 * HEAD_DIM))
        else:
            xp, xs, kv, kv_s = _sb_layer(xp, xs, dims, page_table, cache_sb_kv, l, norm_g[i, 1],
                                         w_in_sb[l], w_out_sb[l], tq)
            n_heads = w_in_sb.shape[-1] // (3 * HEAD_DIM)
            p_sb_kv.append(kv.reshape(b, t, 2, n_heads, HEAD_DIM))
            s_sb_kv.append(kv_s.reshape(db, n_q, 2, n_heads, HEAD_DIM))
        fg = final_norm_g if last else None
        xp = _ffn(xp, norm_g[i, 2], wu[1], wd[1], final_g=fg)
        xs = _ffn(xs, norm_g[i, 2], wu[1], wd[1], final_g=fg)
    return (xp.reshape(b, t, d), xs.reshape(db, n_q, d),
            jnp.stack(p_fox_kv), jnp.stack(p_fox_lf), jnp.stack(p_diff_kv), jnp.stack(p_sb_kv),
            jnp.stack(s_fox_kv), jnp.stack(s_fox_lf), jnp.stack(s_diff_kv), jnp.stack(s_sb_kv))
```

```python
import functools
import math

import jax
import jax.numpy as jnp
from jax import lax
from jax.experimental import pallas as pl
from jax.experimental.pallas import tpu as pltpu

F32 = jnp.float32
BF16 = jnp.bfloat16

HEAD_DIM = 64
H_FOX = 8
H_DIFF = 4
FOX_W = H_FOX * HEAD_DIM
DIFF_W = H_DIFF * 2 * HEAD_DIM
N_BUCKETS = 32
MAX_EXACT = N_BUCKETS // 2
MAX_DISTANCE = 128
RMS_EPS = 1e-5
NEG = -1e30
Q_SCALE = HEAD_DIM ** -0.5

LANES = 128
MXU_DIM = 256
F_PAD = LANES
ATT_W = 4 * LANES
ONES_ROWS = 16
VMEM_LIMIT = 48 * 1024 * 1024
PAGE_STEP_BYTES = 16 * 1024 * 1024
DECODE_VMEM_LIMIT = 2 * PAGE_STEP_BYTES + 24 * 1024 * 1024

_NT = (((1,), (1,)), ((), ()))


def _cparams(*sem, vmem=VMEM_LIMIT):
    return pltpu.CompilerParams(dimension_semantics=sem, vmem_limit_bytes=vmem)


def _rmsnorm(x, g):
    return x * lax.rsqrt(jnp.mean(x * x, axis=-1, keepdims=True) + RMS_EPS) * g


def _log_sigmoid(x):
    return jnp.minimum(x, 0.0) - jnp.log(1.0 + jnp.exp(-jnp.abs(x)))


def _pick_tile(n, target):
    t = min(n, target)
    while n % t:
        t //= 2
    return t


def _ffn_kernel(x_ref, g_ref, wg_ref, wu_ref, wd_ref, *rest, final):
    if final:
        fg_ref, o_ref, h_ref, acc_ref = rest
    else:
        o_ref, h_ref, acc_ref = rest
    j = pl.program_id(1)

    @pl.when(j == 0)
    def _():
        h_ref[...] = _rmsnorm(x_ref[...], g_ref[...]).astype(BF16)
        acc_ref[...] = jnp.zeros_like(acc_ref)

    h = h_ref[...]
    gate = jnp.dot(h, wg_ref[...], preferred_element_type=F32)
    up = jnp.dot(h, wu_ref[...], preferred_element_type=F32)
    act = (gate * jax.nn.sigmoid(gate) * up).astype(BF16)
    acc_ref[...] += jnp.dot(act, wd_ref[...], preferred_element_type=F32)

    @pl.when(j == pl.num_programs(1) - 1)
    def _():
        y = x_ref[...] + 0.5 * acc_ref[...]
        if final:
            y = _rmsnorm(y, fg_ref[...])
        o_ref[...] = y


def _ffn(x, g, w_up, w_down, final_g=None, tm_target=512, tf_target=1408):
    m, d = x.shape
    d_ff = w_down.shape[0]
    tm = _pick_tile(m, tm_target)
    tf = _pick_tile(d_ff, tf_target)
    nf = d_ff // tf
    final = final_g is not None
    in_specs = [
        pl.BlockSpec((tm, d), lambda i, j: (i, 0)),
        pl.BlockSpec((1, d), lambda i, j: (0, 0)),
        pl.BlockSpec((d, tf), lambda i, j: (0, j)),
        pl.BlockSpec((d, tf), lambda i, j: (0, j + nf)),
        pl.BlockSpec((tf, d), lambda i, j: (j, 0)),
    ]
    args = [x, g.reshape(1, d), w_up, w_up, w_down]
    if final:
        in_specs.append(pl.BlockSpec((1, d), lambda i, j: (0, 0)))
        args.append(final_g.reshape(1, d))
    return pl.pallas_call(
        functools.partial(_ffn_kernel, final=final),
        grid=(m // tm, nf),
        in_specs=in_specs,
        out_specs=pl.BlockSpec((tm, d), lambda i, j: (i, 0)),
        out_shape=jax.ShapeDtypeStruct((m, d), F32),
        scratch_shapes=[pltpu.VMEM((tm, d), BF16), pltpu.VMEM((tm, d), F32)],
        compiler_params=_cparams("parallel", "arbitrary"),
        name="half_ffn",
    )(*args)


def _proj_kernel(x_ref, g_ref, wq_ref, *rest, n_kv, with_f):
    kv_w = rest[:n_kv]
    rest = rest[n_kv:]
    if with_f:
        wf_ref, bf_ref = rest[:2]
        rest = rest[2:]
    qkv_ref = rest[0]
    kv_o = rest[1:1 + n_kv]
    h = _rmsnorm(x_ref[...], g_ref[...]).astype(BF16)
    q = jnp.dot(h, wq_ref[...], preferred_element_type=F32) * Q_SCALE
    wq_cols = wq_ref.shape[1]
    qkv_ref[:, :wq_cols] = q.astype(BF16)
    off = wq_cols
    for w_ref, o_ref in zip(kv_w, kv_o):
        kv = jnp.dot(h, w_ref[...], preferred_element_type=F32)
        o_ref[...] = kv
        qkv_ref[:, off:off + kv.shape[1]] = kv.astype(BF16)
        off += kv.shape[1]
    if with_f:
        lf_ref = rest[1 + n_kv]
        f = jnp.dot(h, wf_ref[...], preferred_element_type=F32) + bf_ref[...]
        lf_ref[...] = _log_sigmoid(f)[:, :lf_ref.shape[1]]


def _proj(x, g, wq, kv_ws, wf=None, bf=None, n_f=0, tm_target=512):
    m, d = x.shape
    tm = _pick_tile(m, tm_target)
    with_f = wf is not None
    n_kv = len(kv_ws)
    tot = wq.shape[1] + sum(w.shape[1] for w in kv_ws)
    const = lambda i: (0, 0)
    row = lambda i: (i, 0)
    in_specs = [pl.BlockSpec((tm, d), row), pl.BlockSpec((1, d), const),
                pl.BlockSpec(wq.shape, const)]
    in_specs += [pl.BlockSpec(w.shape, const) for w in kv_ws]
    args = [x, g.reshape(1, d), wq, *kv_ws]
    out_shape = [jax.ShapeDtypeStruct((m, tot), BF16)]
    out_specs = [pl.BlockSpec((tm, tot), row)]
    for w in kv_ws:
        out_shape.append(jax.ShapeDtypeStruct((m, w.shape[1]), F32))
        out_specs.append(pl.BlockSpec((tm, w.shape[1]), row))
    if with_f:
        in_specs += [pl.BlockSpec(wf.shape, const), pl.BlockSpec(bf.shape, const)]
        args += [wf, bf]
        out_shape.append(jax.ShapeDtypeStruct((m, n_f), F32))
        out_specs.append(pl.BlockSpec((tm, n_f), row))
    return pl.pallas_call(
        functools.partial(_proj_kernel, n_kv=n_kv, with_f=with_f),
        grid=(m // tm,),
        in_specs=in_specs,
        out_specs=out_specs,
        out_shape=out_shape,
        compiler_params=_cparams("parallel"),
        name="in_proj",
    )(*args)


def _proj_t_kernel(x_ref, g_ref, wq_ref, wk_ref, wkvt_ref, *rest, ab):
    if ab:
        wvb_ref, wvbt_ref, wf_ref, bf_ref, q_o, k_o, vt_o, kvt_o, kvb_o, lf_o = rest
    else:
        q_o, k_o, vt_o, kvt_o = rest
    h = _rmsnorm(x_ref[...], g_ref[...]).astype(BF16)
    q_o[...] = (jnp.dot(h, wq_ref[...], preferred_element_type=F32) * Q_SCALE).astype(BF16)
    k = jnp.dot(h, wk_ref[...], preferred_element_type=F32)
    k_o[...] = k.astype(BF16)
    kvt = lax.dot_general(wkvt_ref[...], h, _NT, preferred_element_type=F32)
    kvt_o[0] = kvt
    w = kvt.shape[0] // 2
    vt_o[0, 0:w, :] = kvt[w:].astype(BF16)
    if ab:
        wb = wvb_ref.shape[1]
        kvb_o[:, 0:wb] = k[:, k.shape[1] - wb:]
        kvb_o[:, wb:] = jnp.dot(h, wvb_ref[...], preferred_element_type=F32)
        vt_o[0, w:, :] = lax.dot_general(wvbt_ref[...], h, _NT, preferred_element_type=F32).astype(BF16)
        f = jnp.dot(h, wf_ref[...], preferred_element_type=F32) + bf_ref[...]
        lf_o[...] = _log_sigmoid(f)[:, :lf_o.shape[1]]


def _proj_t(x, g, b, t, wq, wk, wkvt, ab_ws=None, n_f=0, tm_target=256):
    m, d = x.shape
    tm = _pick_tile(t, tm_target)
    nt = t // tm
    ab = ab_ws is not None
    const = lambda bi, ti: (0, 0)
    row = lambda bi, ti: (bi * nt + ti, 0)
    seq = lambda bi, ti: (bi, 0, ti)
    in_specs = [pl.BlockSpec((tm, d), row), pl.BlockSpec((1, d), const), pl.BlockSpec(wq.shape, const),
                pl.BlockSpec(wk.shape, const), pl.BlockSpec(wkvt.shape, const)]
    args = [x, g.reshape(1, d), wq, wk, wkvt]
    w2 = wkvt.shape[0]
    vt_rows = w2 // 2 + (ab_ws[1].shape[0] if ab else 0)
    out_shape = [jax.ShapeDtypeStruct((m, wq.shape[1]), BF16), jax.ShapeDtypeStruct((m, wk.shape[1]), BF16),
                 jax.ShapeDtypeStruct((b, vt_rows, t), BF16), jax.ShapeDtypeStruct((b, w2, t), F32)]
    out_specs = [pl.BlockSpec((tm, wq.shape[1]), row), pl.BlockSpec((tm, wk.shape[1]), row),
                 pl.BlockSpec((1, vt_rows, tm), seq), pl.BlockSpec((1, w2, tm), seq)]
    if ab:
        wvb, wvbt, wf, bf = ab_ws
        in_specs += [pl.BlockSpec(a.shape, const) for a in ab_ws]
        args += list(ab_ws)
        out_shape += [jax.ShapeDtypeStruct((m, 2 * wvb.shape[1]), F32), jax.ShapeDtypeStruct((m, n_f), F32)]
        out_specs += [pl.BlockSpec((tm, 2 * wvb.shape[1]), row), pl.BlockSpec((tm, n_f), row)]
    return pl.pallas_call(
        functools.partial(_proj_t_kernel, ab=ab),
        grid=(b, nt),
        in_specs=in_specs,
        out_specs=out_specs,
        out_shape=out_shape,
        compiler_params=_cparams("parallel", "parallel"),
        name="in_proj_t",
    )(*args)


def _outproj_kernel(x_ref, *rest, n_parts):
    o_refs = rest[:n_parts]
    w_refs = rest[n_parts:2 * n_parts]
    y_ref = rest[2 * n_parts]
    acc = x_ref[...]
    for o_ref, w_ref in zip(o_refs, w_refs):
        acc = acc + jnp.dot(o_ref[...].astype(BF16), w_ref[...], preferred_element_type=F32)
    y_ref[...] = acc


def _outproj(x, parts, ws, tm_target=512):
    m, d = x.shape
    tm = _pick_tile(m, tm_target)
    row = lambda i: (i, 0)
    const = lambda i: (0, 0)
    in_specs = [pl.BlockSpec((tm, d), row)]
    in_specs += [pl.BlockSpec((tm, p.shape[1]), row) for p in parts]
    in_specs += [pl.BlockSpec(w.shape, const) for w in ws]
    return pl.pallas_call(
        functools.partial(_outproj_kernel, n_parts=len(parts)),
        grid=(m // tm,),
        in_specs=in_specs,
        out_specs=pl.BlockSpec((tm, d), row),
        out_shape=jax.ShapeDtypeStruct((m, d), F32),
        compiler_params=_cparams("parallel"),
        name="out_proj",
    )(x, *parts, *ws)


def _cumsum_lanes_kernel(x_ref, o_ref):
    x = x_ref[0]
    t = x.shape[1]
    lane = lax.broadcasted_iota(jnp.int32, x.shape, 1)
    d = 1
    while d < t:
        x = x + jnp.where(lane >= d, pltpu.roll(x, d, axis=1), 0.0)
        d *= 2
    o_ref[0] = x


def _cumsum_lanes(x):
    b, h, t = x.shape
    spec = pl.BlockSpec((1, h, t), lambda i: (i, 0, 0))
    return pl.pallas_call(
        _cumsum_lanes_kernel, grid=(b,), in_specs=[spec], out_specs=spec,
        out_shape=jax.ShapeDtypeStruct(x.shape, F32),
        compiler_params=_cparams("parallel"), name="logf_cumsum",
    )(x)


def _bucket(n):
    nf = jnp.maximum(n, MAX_EXACT).astype(F32)
    large = MAX_EXACT + (jnp.log(nf / MAX_EXACT) / math.log(MAX_DISTANCE / MAX_EXACT)
                         * (N_BUCKETS - MAX_EXACT)).astype(jnp.int32)
    return jnp.where(n < MAX_EXACT, n, jnp.minimum(large, N_BUCKETS - 1))


def _bias_kernel(tbl_ref, blk_ref, dec_ref, *, tq, n_q, page):
    h = pl.program_id(0)
    r = lax.broadcasted_iota(jnp.int32, (tq, tq), 0)
    c = lax.broadcasted_iota(jnp.int32, (tq, tq), 1)
    for which, base in enumerate((0, tq)):
        bucket = _bucket(jnp.maximum(c - r + base, 0))
        val = jnp.zeros((tq, tq), F32)
        for b in range(N_BUCKETS):
            val = jnp.where(bucket == b, tbl_ref[b, h], val)
        blk_ref[0, which] = val
    n_rows = n_q * 2 * H_DIFF
    rr = lax.broadcasted_iota(jnp.int32, (n_rows, page), 0)
    jj = lax.broadcasted_iota(jnp.int32, (n_rows, page), 1)
    qi = rr % n_q
    hd = rr // (2 * n_q)
    dists = (page + qi - jj, jnp.maximum(qi - jj, 0), jnp.full((n_rows, page), MAX_DISTANCE, jnp.int32))
    for which, dist in enumerate(dists):
        bucket = _bucket(dist)
        val = jnp.zeros((n_rows, page), F32)
        for b in range(N_BUCKETS):
            tb = jnp.zeros((n_rows, page), F32)
            for hh in range(H_DIFF):
                tb = jnp.where(hd == hh, tbl_ref[b, hh], tb)
            val = jnp.where(bucket == b, tb, val)
        dec_ref[which] = val


def _bias_tables(table, tq, n_q, page):
    n_rows = n_q * 2 * H_DIFF
    return pl.pallas_call(
        functools.partial(_bias_kernel, tq=tq, n_q=n_q, page=page),
        grid=(H_DIFF,),
        in_specs=[pl.BlockSpec(memory_space=pltpu.SMEM)],
        out_specs=[pl.BlockSpec((1, 2, tq, tq), lambda h: (h, 0, 0, 0)),
                   pl.BlockSpec((3, n_rows, page), lambda h: (0, 0, 0))],
        out_shape=[jax.ShapeDtypeStruct((H_DIFF, 2, tq, tq), F32),
                   jax.ShapeDtypeStruct((3, n_rows, page), F32)],
        compiler_params=_cparams("arbitrary"),
        name="rel_bias_tables",
    )(table)


def _half_mask(tq, upper):
    lane = lax.broadcasted_iota(jnp.int32, (tq, LANES), 1)
    return lane >= HEAD_DIM if upper else lane < HEAD_DIM


def _tile_iota(tq):
    row = lax.broadcasted_iota(jnp.int32, (tq, 2 * tq), 0)
    col = lax.broadcasted_iota(jnp.int32, (tq, 2 * tq), 1)
    return row, jnp.where(col >= tq, col - tq, col)


def _split_halves(q):
    zero = jnp.zeros_like(q)
    return jnp.concatenate([jnp.where(_half_mask(q.shape[0], False), q, zero),
                            jnp.where(_half_mask(q.shape[0], True), q, zero)], axis=0)


def _with_ones_rows(vt):
    return jnp.concatenate([vt, jnp.ones((ONES_ROWS, vt.shape[1]), vt.dtype)], axis=0)


def _softmax_chain_step(st, vt, m_old, acc):
    m_new = jnp.maximum(m_old, jnp.max(st, axis=0, keepdims=True))
    alpha = jnp.exp(m_old - m_new)
    pt = jnp.exp(st - m_new).astype(BF16)
    return m_new, alpha * acc + jnp.dot(vt, pt, preferred_element_type=F32)


def _run_tiles(qi, blk, state, diag_first):
    if diag_first:
        state = blk(qi, state, True)
        return lax.fori_loop(0, qi, lambda t, s: blk(qi - 1 - t, s, False), state)
    state = lax.fori_loop(0, qi, lambda kj, s: blk(kj, s, False), state)
    return blk(qi, state, True)


def _init_softmax_state(m_sc, l_sc, acc_sc):
    m_sc[...] = jnp.full_like(m_sc, NEG)
    l_sc[...] = jnp.zeros_like(l_sc)
    acc_sc[...] = jnp.zeros_like(acc_sc)


def _store_pair(o_ref, p, top, bottom):
    pair_t = jnp.concatenate([top, bottom], axis=0)
    o_ref[:, p * LANES:(p + 1) * LANES] = pair_t.T.astype(BF16)


def _fox_attn_kernel(q_ref, k_ref, vt_ref, ft_ref, fc_ref, o_ref, *, tq):
    qi = pl.program_id(2)
    n_heads = ATT_W // HEAD_DIM
    row, col = _tile_iota(tq)
    first = lax.broadcasted_iota(jnp.int32, (tq, 2 * tq), 1) < tq
    qoff = pl.multiple_of(qi * tq, tq)

    def blk(kj, state, masked):
        koff = pl.multiple_of(kj * tq, tq)
        new = []
        for p in range(n_heads // 2):
            pr = slice(p * LANES, (p + 1) * LANES)
            k = k_ref[pl.ds(koff, tq), pr]
            st = lax.dot_general(k, _split_halves(q_ref[:, pr]), _NT, preferred_element_type=F32)
            fq = jnp.concatenate([ft_ref[0, h:h + 1, pl.ds(qoff, tq)] for h in (2 * p, 2 * p + 1)], axis=1)
            fk = jnp.where(first, fc_ref[0, pl.ds(koff, tq), 2 * p:2 * p + 1],
                           fc_ref[0, pl.ds(koff, tq), 2 * p + 1:2 * p + 2])
            st = st + (fq - fk)
            if masked:
                st = jnp.where(row <= col, st, NEG)
            vt = _with_ones_rows(vt_ref[pr, pl.ds(koff, tq)])
            new.extend(_softmax_chain_step(st, vt, state[2 * p], state[2 * p + 1]))
        return tuple(new)

    init = (jnp.full((1, 2 * tq), NEG, F32), jnp.zeros((LANES + ONES_ROWS, 2 * tq), F32)) * (n_heads // 2)
    state = _run_tiles(qi, blk, init, False)
    for p in range(n_heads // 2):
        a = state[2 * p + 1]
        _store_pair(o_ref, p, a[:HEAD_DIM, :tq] / a[LANES:LANES + 1, :tq],
                    a[HEAD_DIM:LANES, tq:] / a[LANES:LANES + 1, tq:])


def _diff_lambda(lp, lam_init):
    return (jnp.exp(jnp.sum(lp[0:1] * lp[1:2], axis=-1, keepdims=True))
            - jnp.exp(jnp.sum(lp[2:3] * lp[3:4], axis=-1, keepdims=True)) + lam_init)


def _diff_attn_kernel(tbl_ref, q_ref, k_ref, vt_ref, bias_ref, lam_ref, g_ref, o_ref, *, tq, lam_init):
    qi = pl.program_id(2)
    row, col = _tile_iota(tq)
    ve = 2 * HEAD_DIM

    def blk(kj, state, masked):
        koff = pl.multiple_of(kj * tq, tq)
        new = []
        for h in range(H_DIFF):
            hs = slice(h * LANES, (h + 1) * LANES)
            q = q_ref[:, hs]
            k = k_ref[pl.ds(koff, tq), hs]
            vt = _with_ones_rows(vt_ref[h * ve:(h + 1) * ve, pl.ds(koff, tq)])
            if masked:
                bias = bias_ref[h, 0]
            else:
                bias = jnp.where(kj == qi - 1, bias_ref[h, 1], tbl_ref[N_BUCKETS - 1, h])
            st = lax.dot_general(k, _split_halves(q), _NT, preferred_element_type=F32)
            st = st + jnp.concatenate([bias, bias], axis=1)
            if masked:
                st = jnp.where(row <= col, st, NEG)
            new.extend(_softmax_chain_step(st, vt, state[2 * h], state[2 * h + 1]))
        return tuple(new)

    init = (jnp.full((1, 2 * tq), NEG, F32), jnp.zeros((ve + ONES_ROWS, 2 * tq), F32)) * H_DIFF
    state = _run_tiles(qi, blk, init, False)
    lam = _diff_lambda(lam_ref[...], lam_init)
    for h in range(H_DIFF):
        a = state[2 * h + 1]
        ot = a[:ve, :tq] / a[ve:ve + 1, :tq] - lam * (a[:ve, tq:] / a[ve:ve + 1, tq:])
        ot = ot * lax.rsqrt(jnp.mean(ot * ot, axis=0, keepdims=True) + RMS_EPS)
        o = ot.T * g_ref[...] * (1.0 - lam_init)
        o_ref[:, h * LANES:(h + 1) * LANES] = o.astype(BF16)


def _strict_upper_sum_matrix(n):
    j = lax.broadcasted_iota(jnp.int32, (n, n), 0)
    s = lax.broadcasted_iota(jnp.int32, (n, n), 1)
    return j > s


def _suffix_sums_2pass(x, mtri):
    hi = x.astype(BF16)
    lo = (x - hi.astype(F32)).astype(BF16)
    return (jnp.dot(hi, mtri, preferred_element_type=F32) + jnp.dot(lo, mtri, preferred_element_type=F32))


def _sb_attn_kernel(q_ref, k_ref, vt_ref, o_ref, *, tq):
    qi = pl.program_id(2)
    n_heads = ATT_W // HEAD_DIM
    row, col = _tile_iota(tq)
    sub = min(tq, MXU_DIM)

    def blk(kj, state, masked):
        koff = pl.multiple_of(kj * tq, tq)
        mtri_t = jnp.where(lax.broadcasted_iota(jnp.int32, (sub, sub), 1)
                           > lax.broadcasted_iota(jnp.int32, (sub, sub), 0), 1.0, 0.0).astype(BF16)
        new = []
        for p in range(n_heads // 2):
            pr = slice(p * LANES, (p + 1) * LANES)
            k = k_ref[pl.ds(koff, tq), pr]
            zt = lax.dot_general(k, _split_halves(q_ref[:, pr]), _NT, preferred_element_type=F32)
            lsz = _log_sigmoid(zt)
            lrest = lsz - zt
            if masked:
                lrest = jnp.where(row < col, lrest, 0.0)
            hi = lrest.astype(BF16)
            lo = (lrest - hi.astype(F32)).astype(BF16)
            hl = jnp.concatenate([hi, lo], axis=1)
            run, acc = state[2 * p], state[2 * p + 1]
            later = run
            parts = []
            for sb in reversed(range(tq // sub)):
                rs = slice(sb * sub, (sb + 1) * sub)
                c2 = jnp.dot(mtri_t, hl[rs], preferred_element_type=F32)
                c = c2[:, :2 * tq] + c2[:, 2 * tq:]
                parts.append(c + later)
                later = later + c[0:1, :] + lrest[sb * sub:sb * sub + 1, :]
            cs = jnp.concatenate(parts[::-1], axis=0)
            a = jnp.exp(lsz + cs)
            if masked:
                a = jnp.where(row < col, a, 0.0)
            vt = vt_ref[pr, pl.ds(koff, tq)]
            new.append(cs[0:1, :] + lrest[0:1, :])
            new.append(acc + jnp.dot(vt, a.astype(BF16), preferred_element_type=F32))
        return tuple(new)

    init = (jnp.zeros((1, 2 * tq), F32), jnp.zeros((LANES, 2 * tq), F32)) * (n_heads // 2)
    state = _run_tiles(qi, blk, init, True)
    for p in range(n_heads // 2):
        acc = state[2 * p + 1]
        _store_pair(o_ref, p, acc[:HEAD_DIM, :tq], acc[HEAD_DIM:, tq:])


def _attn_specs(t, tq, blk0):
    nq = t // tq
    q_spec = pl.BlockSpec((tq, ATT_W), lambda bi, g, qi: (bi * nq + qi, blk0 + g))
    k_spec = pl.BlockSpec((t, ATT_W), lambda bi, g, qi: (bi, blk0 + g))
    vt_spec = pl.BlockSpec((None, ATT_W, t), lambda bi, g, qi: (bi, blk0 + g, 0))
    o_spec = pl.BlockSpec((tq, ATT_W), lambda bi, g, qi: (bi * nq + qi, g))
    return nq, q_spec, k_spec, vt_spec, o_spec


_ATTN_SEM = ("parallel", "parallel", "arbitrary")


def _fox_attn(q, k, vt, ft, fc, b, t, tq, blk0):
    nq, q_spec, k_spec, vt_spec, o_spec = _attn_specs(t, tq, blk0)
    n_heads = ATT_W // HEAD_DIM
    return pl.pallas_call(
        functools.partial(_fox_attn_kernel, tq=tq),
        grid=(b, FOX_W // ATT_W, nq),
        in_specs=[q_spec, k_spec, vt_spec,
                  pl.BlockSpec((1, n_heads, t), lambda bi, g, qi: (bi, g, 0)),
                  pl.BlockSpec((1, t, n_heads), lambda bi, g, qi: (bi, 0, g))],
        out_specs=o_spec,
        out_shape=jax.ShapeDtypeStruct((b * t, FOX_W), BF16),
        compiler_params=_cparams(*_ATTN_SEM),
        name="fox_attn",
    )(q, k, vt, ft, fc)


def _diff_attn(q, k, vt, table, bias_blk, lam_p, subln_g, lam_init, b, t, tq, blk0):
    nq, q_spec, k_spec, vt_spec, o_spec = _attn_specs(t, tq, blk0)
    return pl.pallas_call(
        functools.partial(_diff_attn_kernel, tq=tq, lam_init=lam_init),
        grid=(b, DIFF_W // ATT_W, nq),
        in_specs=[pl.BlockSpec(memory_space=pltpu.SMEM), q_spec, k_spec, vt_spec,
                  pl.BlockSpec(bias_blk.shape, lambda bi, g, qi: (0, 0, 0, 0)),
                  pl.BlockSpec(lam_p.shape, lambda bi, g, qi: (0, 0)),
                  pl.BlockSpec((1, LANES), lambda bi, g, qi: (0, 0))],
        out_specs=o_spec,
        out_shape=jax.ShapeDtypeStruct((b * t, DIFF_W), BF16),
        compiler_params=_cparams(*_ATTN_SEM),
        name="diff_attn",
    )(table, q, k, vt, bias_blk, lam_p, subln_g.reshape(1, LANES))


def _sb_attn(q, k, vt, width, b, t, tq):
    nq, q_spec, k_spec, vt_spec, o_spec = _attn_specs(t, tq, 0)
    return pl.pallas_call(
        functools.partial(_sb_attn_kernel, tq=tq),
        grid=(b, width // ATT_W, nq),
        in_specs=[q_spec, k_spec, vt_spec],
        out_specs=o_spec,
        out_shape=jax.ShapeDtypeStruct((b * t, width), BF16),
        compiler_params=_cparams(*_ATTN_SEM),
        name="sb_attn",
    )(q, k, vt)


def _head_mask(n_heads, width, group):
    hrow = lax.broadcasted_iota(jnp.int32, (n_heads, width), 0)
    hcol = lax.broadcasted_iota(jnp.int32, (n_heads, width), 1) // group
    return hrow == hcol


def _build_wq(q, n_heads):
    n_q, w = q.shape
    mask = _head_mask(n_heads, w, HEAD_DIM)
    parts = [jnp.where(mask, jnp.broadcast_to(q[i:i + 1, :], (n_heads, w)), 0.0) for i in range(n_q)]
    return jnp.concatenate(parts, axis=0).astype(BF16)


def _query_of_row(n_rows, n_heads, width):
    return lax.broadcasted_iota(jnp.int32, (n_rows, width), 0) // n_heads


def _fill_new_page(new_sc, kvnew_ref):
    new_sc[...] = jnp.zeros_like(new_sc)
    new_sc[:, :, 0:kvnew_ref.shape[3]] = kvnew_ref[0]


def _dec_softmax_update(s, vts, m_sc, l_sc, acc_sc):
    m_old = m_sc[...]
    m_new = jnp.maximum(m_old, jnp.max(s, axis=-1, keepdims=True))
    alpha = jnp.exp(m_old - m_new)
    p = jnp.exp(s - m_new)
    l_sc[...] = alpha * l_sc[...] + jnp.sum(p, axis=-1, keepdims=True)
    acc_sc[...] = alpha * acc_sc[...] + lax.dot_general(
        p.astype(BF16), jnp.concatenate(vts, axis=1), _NT, preferred_element_type=F32)
    m_sc[...] = m_new


def _gather_heads(acc, n_q, n_heads, mask):
    rows = [jnp.sum(jnp.where(mask, acc[i * n_heads:(i + 1) * n_heads], 0.0), axis=0, keepdims=True)
            for i in range(n_q)]
    return jnp.concatenate(rows, axis=0)


def _fox_dec_kernel(pt_ref, q_ref, kvnew_ref, lfn_ref, *refs, n_pages):
    del pt_ref
    pages = refs[:n_pages]
    lfs = refs[n_pages:2 * n_pages]
    o_ref = refs[2 * n_pages]
    wq_sc, m_sc, l_sc, acc_sc, car_sc, fq_sc, new_sc = refs[2 * n_pages + 1:]
    p = pl.program_id(1)
    n_q, w = q_ref.shape[1], q_ref.shape[2]
    n_rows = n_q * H_FOX
    page = pages[0].shape[2]

    @pl.when(p == 0)
    def _():
        wq = _build_wq(q_ref[0], H_FOX)
        wq_sc[...] = wq
        lfn = lfn_ref[0]
        run = jnp.zeros((H_FOX, 1), F32)
        cums = []
        for i in range(n_q):
            run = run + lfn[:, i:i + 1]
            cums.append(run)
        fq = jnp.concatenate(cums, axis=0)
        fq_sc[...] = fq
        lane = lax.broadcasted_iota(jnp.int32, (H_FOX, page), 1)
        fnew = jnp.zeros((H_FOX, page), F32)
        for j in range(n_q):
            fnew = jnp.where(lane == j, cums[j], fnew)
        fnew = jnp.concatenate([fnew] * n_q, axis=0)
        _fill_new_page(new_sc, kvnew_ref)
        s = jnp.dot(wq, new_sc[0].astype(BF16), preferred_element_type=F32) + (fq - fnew)
        jj = lax.broadcasted_iota(jnp.int32, (n_rows, page), 1)
        s = jnp.where(jj <= _query_of_row(n_rows, H_FOX, page), s, NEG)
        _init_softmax_state(m_sc, l_sc, acc_sc)
        car_sc[...] = jnp.zeros_like(car_sc)
        _dec_softmax_update(s, [new_sc[1].astype(BF16)], m_sc, l_sc, acc_sc)

    wq = wq_sc[...]
    mtri = jnp.where(_strict_upper_sum_matrix(page), 1.0, 0.0).astype(F32)
    lfx = jnp.concatenate([lf[...] for lf in lfs], axis=0)
    sfx = jnp.dot(lfx, mtri, preferred_element_type=F32, precision=lax.Precision.HIGHEST)
    carry = car_sc[...]
    decay = []
    for g in range(n_pages):
        sl = slice(g * H_FOX, (g + 1) * H_FOX)
        suf = sfx[sl] + carry
        carry = carry + sfx[sl, 0:1] + lfx[sl, 0:1]
        decay.append(jnp.concatenate([suf] * n_q, axis=0))
    car_sc[...] = carry
    kts = jnp.concatenate([pg[0].astype(BF16) for pg in pages], axis=1)
    s = jnp.dot(wq, kts, preferred_element_type=F32) + jnp.concatenate(decay, axis=1) + fq_sc[...]
    _dec_softmax_update(s, [pg[1].astype(BF16) for pg in pages], m_sc, l_sc, acc_sc)

    @pl.when(p == pl.num_programs(1) - 1)
    def _():
        accn = acc_sc[...] / l_sc[...]
        o_ref[0] = _gather_heads(accn, n_q, H_FOX, _head_mask(H_FOX, w, HEAD_DIM))


def _diff_dec_kernel(pt_ref, q_ref, kvnew_ref, dec_ref, lam_ref, g_ref, *refs, n_pages, lam_init):
    del pt_ref
    pages = refs[:n_pages]
    o_ref = refs[n_pages]
    m_sc, l_sc, acc_sc, new_sc = refs[n_pages + 1:]
    p = pl.program_id(1)
    n_q = o_ref.shape[1]
    rph = 2 * n_q
    n_rows = H_DIFF * rph
    stride = 2 * H_DIFF
    page = pages[0].shape[0] // stride

    def head_rows(refs_, first, h):
        return jnp.concatenate([r[pl.ds(first + h, page, stride=stride), :].astype(BF16) for r in refs_], axis=0)

    def scores(refs_):
        parts = []
        for h in range(H_DIFF):
            qh = q_ref[0, h * rph:(h + 1) * rph, :].astype(BF16)
            parts.append(lax.dot_general(qh, head_rows(refs_, 0, h), _NT, preferred_element_type=F32))
        return jnp.concatenate(parts, axis=0)

    def update(s, refs_):
        m_old = m_sc[...]
        m_new = jnp.maximum(m_old, jnp.max(s, axis=-1, keepdims=True))
        alpha = jnp.exp(m_old - m_new)
        pr = jnp.exp(s - m_new)
        l_sc[...] = alpha * l_sc[...] + jnp.sum(pr, axis=-1, keepdims=True)
        acc = alpha * acc_sc[...]
        rows = []
        for h in range(H_DIFF):
            rs = slice(h * rph, (h + 1) * rph)
            rows.append(acc[rs] + jnp.dot(pr[rs].astype(BF16), head_rows(refs_, H_DIFF, h),
                                          preferred_element_type=F32))
        acc_sc[...] = jnp.concatenate(rows, axis=0)
        m_sc[...] = m_new

    @pl.when(p == 0)
    def _():
        new_sc[...] = jnp.zeros_like(new_sc)
        new_sc[0:kvnew_ref.shape[1], :] = kvnew_ref[0]
        s = scores([new_sc]) + dec_ref[1]
        jj = lax.broadcasted_iota(jnp.int32, (n_rows, page), 1)
        ii = lax.broadcasted_iota(jnp.int32, (n_rows, page), 0) % n_q
        s = jnp.where(jj <= ii, s, NEG)
        _init_softmax_state(m_sc, l_sc, acc_sc)
        update(s, [new_sc])

    far = dec_ref[2]
    bias = jnp.concatenate([jnp.where(p == 0, dec_ref[0], far)] + [far] * (n_pages - 1), axis=1)
    update(scores(pages) + bias, pages)

    @pl.when(p == pl.num_programs(1) - 1)
    def _():
        lam = _diff_lambda(lam_ref[...], lam_init)
        accn = acc_sc[...] / l_sc[...]
        outs = []
        for h in range(H_DIFF):
            blk = accn[h * rph:(h + 1) * rph]
            oh = blk[0:n_q] - lam * blk[n_q:rph]
            outs.append(_rmsnorm(oh, g_ref[...]) * (1.0 - lam_init))
        o_ref[0] = jnp.concatenate(outs, axis=1)


def _sb_dec_kernel(pt_ref, q_ref, kvnew_ref, *refs, n_pages, n_heads):
    del pt_ref
    pages = refs[:n_pages]
    o_ref = refs[n_pages]
    wq_sc, acc_sc, run_sc, new_sc = refs[n_pages + 1:]
    p = pl.program_id(1)
    n_q, w = q_ref.shape[1], q_ref.shape[2]
    n_rows = n_q * n_heads
    page = pages[0].shape[2]

    def absorb(wq, kts, vts, strict):
        mtri = jnp.where(_strict_upper_sum_matrix(page), 1.0, 0.0).astype(BF16)
        n = len(kts)
        z = jnp.dot(wq, jnp.concatenate(kts, axis=1), preferred_element_type=F32)
        lsz = _log_sigmoid(z)
        lrest = lsz - z
        if strict is not None:
            lrest = jnp.where(strict, lrest, 0.0)
        lrests = [lrest[:, g * page:(g + 1) * page] for g in range(n)]
        cs_all = _suffix_sums_2pass(jnp.concatenate(lrests, axis=0), mtri)
        run = run_sc[...]
        offs = []
        for g in range(n):
            cs = cs_all[g * n_rows:(g + 1) * n_rows]
            offs.append(cs + run)
            run = run + cs[:, 0:1] + lrests[g][:, 0:1]
        a = jnp.exp(lsz + jnp.concatenate(offs, axis=1))
        if strict is not None:
            a = jnp.where(strict, a, 0.0)
        run_sc[...] = run
        acc_sc[...] = acc_sc[...] + lax.dot_general(a.astype(BF16), jnp.concatenate(vts, axis=1), _NT,
                                                    preferred_element_type=F32)

    @pl.when(p == 0)
    def _():
        wq = _build_wq(q_ref[0], n_heads)
        wq_sc[...] = wq
        _fill_new_page(new_sc, kvnew_ref)
        acc_sc[...] = jnp.zeros_like(acc_sc)
        run_sc[...] = jnp.zeros_like(run_sc)
        jj = lax.broadcasted_iota(jnp.int32, (n_rows, page), 1)
        strict = jj < _query_of_row(n_rows, n_heads, page)
        absorb(wq, [new_sc[0].astype(BF16)], [new_sc[1].astype(BF16)], strict)

    absorb(wq_sc[...], [pg[0].astype(BF16) for pg in pages], [pg[1].astype(BF16) for pg in pages], None)

    @pl.when(p == pl.num_programs(1) - 1)
    def _():
        o_ref[0] = _gather_heads(acc_sc[...], n_q, n_heads, _head_mask(n_heads, w, HEAD_DIM))


def _page_specs(pool, layer, n_steps, n_pages):
    total = n_steps * n_pages
    tail = (0,) * (pool.ndim - 2)
    specs = []
    for g in range(n_pages):
        specs.append(pl.BlockSpec(
            (None, None) + pool.shape[2:],
            lambda b, p, pt, g=g: (layer, pt[b, total - 1 - (p * n_pages + g)]) + tail))
    return specs


def _per_batch_spec(shape):
    return pl.BlockSpec((1,) + shape[1:], lambda b, p, pt: (b,) + (0,) * (len(shape) - 1))


def _const_spec(shape):
    return pl.BlockSpec(shape, lambda b, p, pt: (0,) * len(shape))


def _decode_call(kernel_fn, name, page_table, pool, layer, n_pages, batch_args, const_args, extra_pools,
                 n_q, out_w, scratch):
    db, n_tbl = page_table.shape
    n_steps = n_tbl // n_pages
    in_specs = [_per_batch_spec(a.shape) for a in batch_args] + [_const_spec(a.shape) for a in const_args]
    args = list(batch_args) + list(const_args)
    for pl_arr in (pool,) + tuple(extra_pools):
        in_specs += _page_specs(pl_arr, layer, n_steps, n_pages)
        args += [pl_arr] * n_pages
    return pl.pallas_call(
        kernel_fn,
        grid_spec=pltpu.PrefetchScalarGridSpec(
            num_scalar_prefetch=1, grid=(db, n_steps), in_specs=in_specs,
            out_specs=pl.BlockSpec((1, n_q, out_w), lambda b, p, pt: (b, 0, 0)),
            scratch_shapes=scratch),
        out_shape=jax.ShapeDtypeStruct((db, n_q, out_w), F32),
        compiler_params=_cparams("parallel", "arbitrary", vmem=DECODE_VMEM_LIMIT),
        name=name,
    )(page_table, *args)


def _pages_per_step(page_table, pool):
    page_bytes = math.prod(pool.shape[2:]) * pool.dtype.itemsize
    return _pick_tile(page_table.shape[1], max(1, PAGE_STEP_BYTES // page_bytes))


def _transposed_pool(cache):
    l, n_pool, page, two, h, dd = cache.shape
    return cache.transpose(0, 1, 3, 4, 5, 2).reshape(l, n_pool, two, h * dd, page)


def _transposed_new(kv_s, db, n_q, width):
    return kv_s.reshape(db, n_q, 2, width).transpose(0, 2, 3, 1)


def _ab_layer(xp, xs, dims, page_table, cache_kv_a, cache_lf, cache_kv_b, layer, g, w_in, b_f, lam_p,
              subln_g, w_out, table, bias_blk, dec_bias, lam_init, tq):
    b, t, db, n_q = dims
    splits = [0, FOX_W, 2 * FOX_W, 3 * FOX_W, 3 * FOX_W + H_FOX, 3 * FOX_W + H_FOX + DIFF_W,
              3 * FOX_W + H_FOX + 2 * DIFF_W, 3 * FOX_W + H_FOX + 3 * DIFF_W]
    q_a, k_a, v_a, f_a, q_b, k_b, v_b = [w_in[:, splits[i]:splits[i + 1]] for i in range(7)]
    wq = jnp.concatenate([q_a, q_b], axis=1).astype(BF16)
    wkva = jnp.concatenate([k_a, v_a], axis=1).astype(BF16)
    wkvb = jnp.concatenate([k_b, v_b], axis=1).astype(BF16)
    wf = jnp.pad(f_a, ((0, 0), (0, F_PAD - H_FOX))).astype(BF16)
    bf = jnp.pad(b_f, (0, F_PAD - H_FOX)).reshape(1, F_PAD)
    wo_a = w_out[:FOX_W].astype(BF16)
    wo_b = w_out[FOX_W:].astype(BF16)
    wk = jnp.concatenate([k_a, k_b], axis=1).astype(BF16)
    q, k, vt, kvat, kvb, lf = _proj_t(xp, g, b, t, wq, wk, wkva.T,
                                      (v_b.astype(BF16), v_b.T.astype(BF16), wf, bf), H_FOX)
    ft = _cumsum_lanes(lf.reshape(b, t, H_FOX).transpose(0, 2, 1))
    fc = ft.transpose(0, 2, 1)
    o_a = _fox_attn(q, k, vt, ft, fc, b, t, tq, 0)
    o_b = _diff_attn(q, k, vt, table, bias_blk, lam_p, subln_g, lam_init, b, t, tq, FOX_W // ATT_W)
    xp = _outproj(xp, [o_a, o_b], [wo_a, wo_b])
    kva = kvat.reshape(b, 2, H_FOX, HEAD_DIM, t).transpose(0, 4, 1, 2, 3)

    qkv_s, kva_s, kvb_s, lf_s = _proj(xs, g, wq, [wkva, wkvb], wf, bf, H_FOX)
    q_s = qkv_s[:, :FOX_W + DIFF_W].astype(F32).reshape(db, n_q, FOX_W + DIFF_W)
    lfn = lf_s.reshape(db, n_q, H_FOX).transpose(0, 2, 1)
    n_pool, page = cache_kv_a.shape[1], cache_kv_a.shape[2]
    pool_a = _transposed_pool(cache_kv_a)
    pool_lf = cache_lf.transpose(0, 1, 3, 2)
    pool_b = cache_kv_b.reshape(cache_kv_b.shape[0], n_pool, page * 2 * H_DIFF, 2 * HEAD_DIM)
    n_pages = _pages_per_step(page_table, pool_a)
    rows_a = n_q * H_FOX
    o_a_s = _decode_call(
        functools.partial(_fox_dec_kernel, n_pages=n_pages), "fox_decode", page_table, pool_a, layer, n_pages,
        [q_s[:, :, :FOX_W], _transposed_new(kva_s, db, n_q, FOX_W), lfn], [], [pool_lf], n_q, FOX_W,
        [pltpu.VMEM((rows_a, FOX_W), BF16), pltpu.VMEM((rows_a, 1), F32), pltpu.VMEM((rows_a, 1), F32),
         pltpu.VMEM((rows_a, FOX_W), F32), pltpu.VMEM((H_FOX, 1), F32), pltpu.VMEM((rows_a, 1), F32),
         pltpu.VMEM((2, FOX_W, page), F32)])
    qb = q_s[:, :, FOX_W:].reshape(db, n_q, H_DIFF, 2, HEAD_DIM).transpose(0, 2, 3, 1, 4)
    q_rows = (qb[:, :, :, :, None, :] * jnp.eye(2, dtype=F32)[None, None, :, None, :, None]).reshape(
        db, H_DIFF * 2 * n_q, 2 * HEAD_DIM)
    rows_b = n_q * 2 * H_DIFF
    o_b_s = _decode_call(
        functools.partial(_diff_dec_kernel, n_pages=n_pages, lam_init=lam_init), "diff_decode", page_table,
        pool_b, layer, n_pages,
        [q_rows, kvb_s.reshape(db, n_q * 2 * H_DIFF, 2 * HEAD_DIM)],
        [dec_bias, lam_p, subln_g.reshape(1, LANES)], [], n_q, DIFF_W,
        [pltpu.VMEM((rows_b, 1), F32), pltpu.VMEM((rows_b, 1), F32), pltpu.VMEM((rows_b, 2 * HEAD_DIM), F32),
         pltpu.VMEM((page * 2 * H_DIFF, 2 * HEAD_DIM), F32)])
    xs = _outproj(xs, [o_a_s.reshape(db * n_q, FOX_W), o_b_s.reshape(db * n_q, DIFF_W)], [wo_a, wo_b])
    return xp, xs, (kva, lf, kvb), (kva_s, lf_s, kvb_s)


def _sb_layer(xp, xs, dims, page_table, cache_kv, layer, g, w_in, w_out, tq):
    b, t, db, n_q = dims
    width = w_in.shape[1] // 3
    n_heads = width // HEAD_DIM
    wq = w_in[:, :width].astype(BF16)
    wkv = w_in[:, width:].astype(BF16)
    wo = w_out.astype(BF16)

    q, k, vt, kvt = _proj_t(xp, g, b, t, wq, wkv[:, :width], wkv.T)
    o = _sb_attn(q, k, vt, width, b, t, tq)
    xp = _outproj(xp, [o], [wo])
    kv = kvt.reshape(b, 2, n_heads, HEAD_DIM, t).transpose(0, 4, 1, 2, 3)

    qkv_s, kv_s = _proj(xs, g, wq, [wkv])
    q_s = qkv_s[:, :width].astype(F32).reshape(db, n_q, width)
    page = cache_kv.shape[2]
    pool = _transposed_pool(cache_kv)
    n_pages = _pages_per_step(page_table, pool)
    rows = n_q * n_heads
    o_s = _decode_call(
        functools.partial(_sb_dec_kernel, n_pages=n_pages, n_heads=n_heads), "sb_decode", page_table, pool,
        layer, n_pages, [q_s, _transposed_new(kv_s, db, n_q, width)], [], [], n_q, width,
        [pltpu.VMEM((rows, width), BF16), pltpu.VMEM((rows, width), F32), pltpu.VMEM((rows, 1), F32),
         pltpu.VMEM((2, width, page), F32)])
    xs = _outproj(xs, [o_s.reshape(db * n_q, width)], [wo])
    return xp, xs, kv, kv_s


def kernel(x_prompt, x_sample, cache_fox_kv, cache_fox_logf, cache_diff_kv, cache_sb_kv, page_table, rel_bias_table, norm_g, final_norm_g, w_ffn_up, w_ffn_down, w_in_ab, b_forget, diff_lambda, diff_subln_g, w_out_ab, w_in_sb, w_out_sb):
    b, t, d = x_prompt.shape
    db, n_q, _ = x_sample.shape
    depth = norm_g.shape[0]
    page = cache_fox_kv.shape[2]
    assert page >= MAX_DISTANCE and page % LANES == 0 and n_q <= page
    tq = _pick_tile(t, 512)
    assert tq >= MAX_DISTANCE
    dims = (b, t, db, n_q)
    xp = x_prompt.reshape(b * t, d)
    xs = x_sample.reshape(db * n_q, d)
    bias_blk, dec_bias = _bias_tables(rel_bias_table, tq, n_q, page)

    p_fox_kv, p_fox_lf, p_diff_kv, p_sb_kv = [], [], [], []
    s_fox_kv, s_fox_lf, s_diff_kv, s_sb_kv = [], [], [], []
    for i in range(depth):
        l = i // 2
        last = i == depth - 1
        wu = w_ffn_up[i].astype(BF16)
        wd = w_ffn_down[i].astype(BF16)
        xp = _ffn(xp, norm_g[i, 0], wu[0], wd[0])
        xs = _ffn(xs, norm_g[i, 0], wu[0], wd[0])
        if i % 2 == 0:
            lam_init = 0.8 - 0.6 * math.exp(-0.3 * i)
            xp, xs, (kva, lf, kvb), (kva_s, lf_s, kvb_s) = _ab_layer(
                xp, xs, dims, page_table, cache_fox_kv, cache_fox_logf, cache_diff_kv, l, norm_g[i, 1],
                w_in_ab[l], b_forget[l], diff_lambda[l], diff_subln_g[l], w_out_ab[l], rel_bias_table,
                bias_blk, dec_bias, lam_init, tq)
            p_fox_kv.append(kva.reshape(b, t, 2, H_FOX, HEAD_DIM))
            p_fox_lf.append(lf.reshape(b, t, H_FOX))
            p_diff_kv.append(kvb.reshape(b, t, 2, H_DIFF, 2 * HEAD_DIM))
            s_fox_kv.append(kva_s.reshape(db, n_q, 2, H_FOX, HEAD_DIM))
            s_fox_lf.append(lf_s.reshape(db, n_q, H_FOX))
            s_diff_kv.append(kvb_s.reshape(db, n_q, 2, H_DIFF, 2 * HEAD_DIM))
        else:
            xp, xs, kv, kv_s = _sb_layer(xp, xs, dims, page_table, cache_sb_kv, l, norm_g[i, 1],
                                         w_in_sb[l], w_out_sb[l], tq)
            n_heads = w_in_sb.shape[-1] // (3 * HEAD_DIM)
            p_sb_kv.append(kv.reshape(b, t, 2, n_heads, HEAD_DIM))
            s_sb_kv.append(kv_s.reshape(db, n_q, 2, n_heads, HEAD_DIM))
        fg = final_norm_g if last else None
        xp = _ffn(xp, norm_g[i, 2], wu[1], wd[1], final_g=fg)
        xs = _ffn(xs, norm_g[i, 2], wu[1], wd[1], final_g=fg)
    return (xp.reshape(b, t, d), xs.reshape(db, n_q, d),
            jnp.stack(p_fox_kv), jnp.stack(p_fox_lf), jnp.stack(p_diff_kv), jnp.stack(p_sb_kv),
            jnp.stack(s_fox_kv), jnp.stack(s_fox_lf), jnp.stack(s_diff_kv), jnp.stack(s_sb_kv))
```

```python
import functools
import math

import jax
import jax.numpy as jnp
from jax import lax
from jax.experimental import pallas as pl
from jax.experimental.pallas import tpu as pltpu

F32 = jnp.float32
BF16 = jnp.bfloat16

HEAD_DIM = 64
H_FOX = 8
H_DIFF = 4
FOX_W = H_FOX * HEAD_DIM
DIFF_W = H_DIFF * 2 * HEAD_DIM
N_BUCKETS = 32
MAX_EXACT = N_BUCKETS // 2
MAX_DISTANCE = 128
RMS_EPS = 1e-5
NEG = -1e30
Q_SCALE = HEAD_DIM ** -0.5

LANES = 128
MXU_DIM = 256
F_PAD = LANES
ATT_W = 4 * LANES
ONES_ROWS = 16
VMEM_LIMIT = 48 * 1024 * 1024
PAGE_STEP_BYTES = 16 * 1024 * 1024
DECODE_VMEM_LIMIT = 2 * PAGE_STEP_BYTES + 24 * 1024 * 1024

_NT = (((1,), (1,)), ((), ()))


def _cparams(*sem, vmem=VMEM_LIMIT):
    return pltpu.CompilerParams(dimension_semantics=sem, vmem_limit_bytes=vmem)


def _rmsnorm(x, g):
    return x * lax.rsqrt(jnp.mean(x * x, axis=-1, keepdims=True) + RMS_EPS) * g


def _log_sigmoid(x):
    return jnp.minimum(x, 0.0) - jnp.log(1.0 + jnp.exp(-jnp.abs(x)))


def _pick_tile(n, target):
    t = min(n, target)
    while n % t:
        t //= 2
    return t


def _ffn_kernel(x_ref, g_ref, wup_ref, wd_ref, *rest, final, tf):
    if final:
        fg_ref, o_ref = rest
    else:
        (o_ref,) = rest
    d_ff = wd_ref.shape[0]
    x = x_ref[...]
    h = _rmsnorm(x, g_ref[...]).astype(BF16)
    y = x
    for c in range(d_ff // tf):
        gate = jnp.dot(h, wup_ref[:, c * tf:(c + 1) * tf], preferred_element_type=F32)
        up = jnp.dot(h, wup_ref[:, d_ff + c * tf:d_ff + (c + 1) * tf], preferred_element_type=F32)
        act = (gate * jax.nn.sigmoid(gate) * up).astype(BF16)
        y = y + 0.5 * jnp.dot(act, wd_ref[c * tf:(c + 1) * tf, :], preferred_element_type=F32)
    if final:
        y = _rmsnorm(y, fg_ref[...])
    o_ref[...] = y


def _ffn(x, g, w_up, w_down, final_g=None, tm_target=512, tf_target=1408):
    m, d = x.shape
    d_ff = w_down.shape[0]
    tm = _pick_tile(m, tm_target)
    tf = _pick_tile(d_ff, tf_target)
    final = final_g is not None
    const = lambda i: (0, 0)
    resident = dict(pipeline_mode=pl.Buffered(1))
    in_specs = [
        pl.BlockSpec((tm, d), lambda i: (i, 0)),
        pl.BlockSpec((1, d), const),
        pl.BlockSpec(w_up.shape, const, **resident),
        pl.BlockSpec(w_down.shape, const, **resident),
    ]
    args = [x, g.reshape(1, d), w_up, w_down]
    if final:
        in_specs.append(pl.BlockSpec((1, d), const))
        args.append(final_g.reshape(1, d))
    return pl.pallas_call(
        functools.partial(_ffn_kernel, final=final, tf=tf),
        grid=(m // tm,),
        in_specs=in_specs,
        out_specs=pl.BlockSpec((tm, d), lambda i: (i, 0)),
        out_shape=jax.ShapeDtypeStruct((m, d), F32),
        compiler_params=_cparams("parallel"),
        name="half_ffn",
    )(*args)


def _proj_kernel(x_ref, g_ref, wq_ref, *rest, n_kv, with_f):
    kv_w = rest[:n_kv]
    rest = rest[n_kv:]
    if with_f:
        wf_ref, bf_ref = rest[:2]
        rest = rest[2:]
    qkv_ref = rest[0]
    kv_o = rest[1:1 + n_kv]
    h = _rmsnorm(x_ref[...], g_ref[...]).astype(BF16)
    q = jnp.dot(h, wq_ref[...], preferred_element_type=F32) * Q_SCALE
    wq_cols = wq_ref.shape[1]
    qkv_ref[:, :wq_cols] = q.astype(BF16)
    off = wq_cols
    for w_ref, o_ref in zip(kv_w, kv_o):
        kv = jnp.dot(h, w_ref[...], preferred_element_type=F32)
        o_ref[...] = kv
        qkv_ref[:, off:off + kv.shape[1]] = kv.astype(BF16)
        off += kv.shape[1]
    if with_f:
        lf_ref = rest[1 + n_kv]
        f = jnp.dot(h, wf_ref[...], preferred_element_type=F32) + bf_ref[...]
        lf_ref[...] = _log_sigmoid(f)[:, :lf_ref.shape[1]]


def _proj(x, g, wq, kv_ws, wf=None, bf=None, n_f=0, tm_target=512):
    m, d = x.shape
    tm = _pick_tile(m, tm_target)
    with_f = wf is not None
    n_kv = len(kv_ws)
    tot = wq.shape[1] + sum(w.shape[1] for w in kv_ws)
    const = lambda i: (0, 0)
    row = lambda i: (i, 0)
    in_specs = [pl.BlockSpec((tm, d), row), pl.BlockSpec((1, d), const),
                pl.BlockSpec(wq.shape, const)]
    in_specs += [pl.BlockSpec(w.shape, const) for w in kv_ws]
    args = [x, g.reshape(1, d), wq, *kv_ws]
    out_shape = [jax.ShapeDtypeStruct((m, tot), BF16)]
    out_specs = [pl.BlockSpec((tm, tot), row)]
    for w in kv_ws:
        out_shape.append(jax.ShapeDtypeStruct((m, w.shape[1]), F32))
        out_specs.append(pl.BlockSpec((tm, w.shape[1]), row))
    if with_f:
        in_specs += [pl.BlockSpec(wf.shape, const), pl.BlockSpec(bf.shape, const)]
        args += [wf, bf]
        out_shape.append(jax.ShapeDtypeStruct((m, n_f), F32))
        out_specs.append(pl.BlockSpec((tm, n_f), row))
    return pl.pallas_call(
        functools.partial(_proj_kernel, n_kv=n_kv, with_f=with_f),
        grid=(m // tm,),
        in_specs=in_specs,
        out_specs=out_specs,
        out_shape=out_shape,
        compiler_params=_cparams("parallel"),
        name="in_proj",
    )(*args)


def _proj_t_kernel(x_ref, g_ref, wq_ref, wk_ref, wkvt_ref, *rest, ab):
    if ab:
        wvb_ref, wvbt_ref, wf_ref, bf_ref, q_o, k_o, vt_o, kvt_o, kvb_o, lf_o = rest
    else:
        q_o, k_o, vt_o, kvt_o = rest
    h = _rmsnorm(x_ref[...], g_ref[...]).astype(BF16)
    q_o[...] = (jnp.dot(h, wq_ref[...], preferred_element_type=F32) * Q_SCALE).astype(BF16)
    k = jnp.dot(h, wk_ref[...], preferred_element_type=F32)
    k_o[...] = k.astype(BF16)
    kvt = lax.dot_general(wkvt_ref[...], h, _NT, preferred_element_type=F32)
    kvt_o[0] = kvt
    w = kvt.shape[0] // 2
    vt_o[0, 0:w, :] = kvt[w:].astype(BF16)
    if ab:
        wb = wvb_ref.shape[1]
        vb = jnp.dot(h, wvb_ref[...], preferred_element_type=F32)
        nh = wb // LANES
        for j in range(nh):
            kvb_o[:, j, :] = k[:, k.shape[1] - wb + j * LANES:k.shape[1] - wb + (j + 1) * LANES]
            kvb_o[:, nh + j, :] = vb[:, j * LANES:(j + 1) * LANES]
        vt_o[0, w:, :] = lax.dot_general(wvbt_ref[...], h, _NT, preferred_element_type=F32).astype(BF16)
        f = jnp.dot(h, wf_ref[...], preferred_element_type=F32) + bf_ref[...]
        lf_o[...] = _log_sigmoid(f)[:, :lf_o.shape[1]]


def _proj_t(x, g, b, t, wq, wk, wkvt, ab_ws=None, n_f=0, tm_target=256):
    m, d = x.shape
    tm = _pick_tile(t, tm_target)
    nt = t // tm
    ab = ab_ws is not None
    const = lambda bi, ti: (0, 0)
    row = lambda bi, ti: (bi * nt + ti, 0)
    seq = lambda bi, ti: (bi, 0, ti)
    in_specs = [pl.BlockSpec((tm, d), row), pl.BlockSpec((1, d), const), pl.BlockSpec(wq.shape, const),
                pl.BlockSpec(wk.shape, const), pl.BlockSpec(wkvt.shape, const)]
    args = [x, g.reshape(1, d), wq, wk, wkvt]
    w2 = wkvt.shape[0]
    vt_rows = w2 // 2 + (ab_ws[1].shape[0] if ab else 0)
    out_shape = [jax.ShapeDtypeStruct((m, wq.shape[1]), BF16), jax.ShapeDtypeStruct((m, wk.shape[1]), BF16),
                 jax.ShapeDtypeStruct((b, vt_rows, t), BF16), jax.ShapeDtypeStruct((b, w2, t), F32)]
    out_specs = [pl.BlockSpec((tm, wq.shape[1]), row), pl.BlockSpec((tm, wk.shape[1]), row),
                 pl.BlockSpec((1, vt_rows, tm), seq), pl.BlockSpec((1, w2, tm), seq)]
    if ab:
        wvb, wvbt, wf, bf = ab_ws
        in_specs += [pl.BlockSpec(a.shape, const) for a in ab_ws]
        args += list(ab_ws)
        kvb_rows = 2 * wvb.shape[1] // LANES
        out_shape += [jax.ShapeDtypeStruct((m, kvb_rows, LANES), F32), jax.ShapeDtypeStruct((m, n_f), F32)]
        out_specs += [pl.BlockSpec((tm, kvb_rows, LANES), lambda bi, ti: (bi * nt + ti, 0, 0)),
                      pl.BlockSpec((tm, n_f), row)]
    return pl.pallas_call(
        functools.partial(_proj_t_kernel, ab=ab),
        grid=(b, nt),
        in_specs=in_specs,
        out_specs=out_specs,
        out_shape=out_shape,
        compiler_params=_cparams("parallel", "parallel"),
        name="in_proj_t",
    )(*args)


def _outproj_kernel(x_ref, *rest, n_parts):
    o_refs = rest[:n_parts]
    w_refs = rest[n_parts:2 * n_parts]
    y_ref = rest[2 * n_parts]
    acc = x_ref[...]
    for o_ref, w_ref in zip(o_refs, w_refs):
        acc = acc + jnp.dot(o_ref[...].astype(BF16), w_ref[...], preferred_element_type=F32)
    y_ref[...] = acc


def _outproj(x, parts, ws, tm_target=512):
    m, d = x.shape
    tm = _pick_tile(m, tm_target)
    row = lambda i: (i, 0)
    const = lambda i: (0, 0)
    in_specs = [pl.BlockSpec((tm, d), row)]
    in_specs += [pl.BlockSpec((tm, p.shape[1]), row) for p in parts]
    in_specs += [pl.BlockSpec(w.shape, const) for w in ws]
    return pl.pallas_call(
        functools.partial(_outproj_kernel, n_parts=len(parts)),
        grid=(m // tm,),
        in_specs=in_specs,
        out_specs=pl.BlockSpec((tm, d), row),
        out_shape=jax.ShapeDtypeStruct((m, d), F32),
        compiler_params=_cparams("parallel"),
        name="out_proj",
    )(x, *parts, *ws)


def _cumsum_lanes_kernel(x_ref, o_ref):
    x = x_ref[0]
    t = x.shape[1]
    lane = lax.broadcasted_iota(jnp.int32, x.shape, 1)
    d = 1
    while d < t:
        x = x + jnp.where(lane >= d, pltpu.roll(x, d, axis=1), 0.0)
        d *= 2
    o_ref[0] = x


def _cumsum_lanes(x):
    b, h, t = x.shape
    spec = pl.BlockSpec((1, h, t), lambda i: (i, 0, 0))
    return pl.pallas_call(
        _cumsum_lanes_kernel, grid=(b,), in_specs=[spec], out_specs=spec,
        out_shape=jax.ShapeDtypeStruct(x.shape, F32),
        compiler_params=_cparams("parallel"), name="logf_cumsum",
    )(x)


def _bucket(n):
    nf = jnp.maximum(n, MAX_EXACT).astype(F32)
    large = MAX_EXACT + (jnp.log(nf / MAX_EXACT) / math.log(MAX_DISTANCE / MAX_EXACT)
                         * (N_BUCKETS - MAX_EXACT)).astype(jnp.int32)
    return jnp.where(n < MAX_EXACT, n, jnp.minimum(large, N_BUCKETS - 1))


def _bias_kernel(tbl_ref, blk_ref, dec_ref, *, tq, n_q, page):
    h = pl.program_id(0)
    r = lax.broadcasted_iota(jnp.int32, (tq, tq), 0)
    c = lax.broadcasted_iota(jnp.int32, (tq, tq), 1)
    for which, base in enumerate((0, tq)):
        bucket = _bucket(jnp.maximum(c - r + base, 0))
        val = jnp.zeros((tq, tq), F32)
        for b in range(N_BUCKETS):
            val = jnp.where(bucket == b, tbl_ref[b, h], val)
        blk_ref[0, which] = val
    n_rows = n_q * 2 * H_DIFF
    rr = lax.broadcasted_iota(jnp.int32, (n_rows, page), 0)
    jj = lax.broadcasted_iota(jnp.int32, (n_rows, page), 1)
    qi = rr % n_q
    hd = rr // (2 * n_q)
    dists = (page + qi - jj, jnp.maximum(qi - jj, 0), jnp.full((n_rows, page), MAX_DISTANCE, jnp.int32))
    for which, dist in enumerate(dists):
        bucket = _bucket(dist)
        val = jnp.zeros((n_rows, page), F32)
        for b in range(N_BUCKETS):
            tb = jnp.zeros((n_rows, page), F32)
            for hh in range(H_DIFF):
                tb = jnp.where(hd == hh, tbl_ref[b, hh], tb)
            val = jnp.where(bucket == b, tb, val)
        dec_ref[which] = val


def _bias_tables(table, tq, n_q, page):
    n_rows = n_q * 2 * H_DIFF
    return pl.pallas_call(
        functools.partial(_bias_kernel, tq=tq, n_q=n_q, page=page),
        grid=(H_DIFF,),
        in_specs=[pl.BlockSpec(memory_space=pltpu.SMEM)],
        out_specs=[pl.BlockSpec((1, 2, tq, tq), lambda h: (h, 0, 0, 0)),
                   pl.BlockSpec((3, n_rows, page), lambda h: (0, 0, 0))],
        out_shape=[jax.ShapeDtypeStruct((H_DIFF, 2, tq, tq), F32),
                   jax.ShapeDtypeStruct((3, n_rows, page), F32)],
        compiler_params=_cparams("arbitrary"),
        name="rel_bias_tables",
    )(table)


def _half_mask(tq, upper):
    lane = lax.broadcasted_iota(jnp.int32, (tq, LANES), 1)
    return lane >= HEAD_DIM if upper else lane < HEAD_DIM


def _tile_iota(tq):
    row = lax.broadcasted_iota(jnp.int32, (tq, 2 * tq), 0)
    col = lax.broadcasted_iota(jnp.int32, (tq, 2 * tq), 1)
    return row, jnp.where(col >= tq, col - tq, col)


def _split_halves(q):
    zero = jnp.zeros_like(q)
    return jnp.concatenate([jnp.where(_half_mask(q.shape[0], False), q, zero),
                            jnp.where(_half_mask(q.shape[0], True), q, zero)], axis=0)


def _with_ones_rows(vt):
    return jnp.concatenate([vt, jnp.ones((ONES_ROWS, vt.shape[1]), vt.dtype)], axis=0)


def _softmax_chain_step(st, vt, m_old, acc):
    m_new = jnp.maximum(m_old, jnp.max(st, axis=0, keepdims=True))
    alpha = jnp.exp(m_old - m_new)
    pt = jnp.exp(st - m_new).astype(BF16)
    return m_new, alpha * acc + jnp.dot(vt, pt, preferred_element_type=F32)


def _run_tiles(qi, blk, state, diag_first):
    if diag_first:
        state = blk(qi, state, True)
        return lax.fori_loop(0, qi, lambda t, s: blk(qi - 1 - t, s, False), state)
    state = lax.fori_loop(0, qi, lambda kj, s: blk(kj, s, False), state)
    return blk(qi, state, True)


def _init_softmax_state(m_sc, l_sc, acc_sc):
    m_sc[...] = jnp.full_like(m_sc, NEG)
    l_sc[...] = jnp.zeros_like(l_sc)
    acc_sc[...] = jnp.zeros_like(acc_sc)


def _store_pair(o_ref, p, top, bottom):
    pair_t = jnp.concatenate([top, bottom], axis=0)
    o_ref[:, p * LANES:(p + 1) * LANES] = pair_t.T.astype(BF16)


def _fox_attn_kernel(q_ref, k_ref, vt_ref, ft_ref, fc_ref, o_ref, *, tq):
    qi = pl.program_id(2)
    n_heads = ATT_W // HEAD_DIM
    row, col = _tile_iota(tq)
    first = lax.broadcasted_iota(jnp.int32, (tq, 2 * tq), 1) < tq
    qoff = pl.multiple_of(qi * tq, tq)

    def blk(kj, state, masked):
        koff = pl.multiple_of(kj * tq, tq)
        new = []
        for p in range(n_heads // 2):
            pr = slice(p * LANES, (p + 1) * LANES)
            k = k_ref[pl.ds(koff, tq), pr]
            st = lax.dot_general(k, _split_halves(q_ref[:, pr]), _NT, preferred_element_type=F32)
            fq = jnp.concatenate([ft_ref[0, h:h + 1, pl.ds(qoff, tq)] for h in (2 * p, 2 * p + 1)], axis=1)
            fk = jnp.where(first, fc_ref[0, pl.ds(koff, tq), 2 * p:2 * p + 1],
                           fc_ref[0, pl.ds(koff, tq), 2 * p + 1:2 * p + 2])
            st = st + (fq - fk)
            if masked:
                st = jnp.where(row <= col, st, NEG)
            vt = _with_ones_rows(vt_ref[pr, pl.ds(koff, tq)])
            new.extend(_softmax_chain_step(st, vt, state[2 * p], state[2 * p + 1]))
        return tuple(new)

    init = (jnp.full((1, 2 * tq), NEG, F32), jnp.zeros((LANES + ONES_ROWS, 2 * tq), F32)) * (n_heads // 2)
    state = _run_tiles(qi, blk, init, False)
    for p in range(n_heads // 2):
        a = state[2 * p + 1]
        _store_pair(o_ref, p, a[:HEAD_DIM, :tq] / a[LANES:LANES + 1, :tq],
                    a[HEAD_DIM:LANES, tq:] / a[LANES:LANES + 1, tq:])


def _diff_lambda(lp, lam_init):
    return (jnp.exp(jnp.sum(lp[0:1] * lp[1:2], axis=-1, keepdims=True))
            - jnp.exp(jnp.sum(lp[2:3] * lp[3:4], axis=-1, keepdims=True)) + lam_init)


def _diff_attn_kernel(tbl_ref, q_ref, k_ref, vt_ref, bias_ref, lam_ref, g_ref, o_ref, *, tq, lam_init):
    qi = pl.program_id(2)
    row, col = _tile_iota(tq)
    ve = 2 * HEAD_DIM

    def blk(kj, state, masked):
        koff = pl.multiple_of(kj * tq, tq)
        new = []
        for h in range(H_DIFF):
            hs = slice(h * LANES, (h + 1) * LANES)
            q = q_ref[:, hs]
            k = k_ref[pl.ds(koff, tq), hs]
            vt = _with_ones_rows(vt_ref[h * ve:(h + 1) * ve, pl.ds(koff, tq)])
            if masked:
                bias = bias_ref[h, 0]
            else:
                bias = jnp.where(kj == qi - 1, bias_ref[h, 1], tbl_ref[N_BUCKETS - 1, h])
            st = lax.dot_general(k, _split_halves(q), _NT, preferred_element_type=F32)
            st = st + jnp.concatenate([bias, bias], axis=1)
            if masked:
                st = jnp.where(row <= col, st, NEG)
            new.extend(_softmax_chain_step(st, vt, state[2 * h], state[2 * h + 1]))
        return tuple(new)

    init = (jnp.full((1, 2 * tq), NEG, F32), jnp.zeros((ve + ONES_ROWS, 2 * tq), F32)) * H_DIFF
    state = _run_tiles(qi, blk, init, False)
    lam = _diff_lambda(lam_ref[...], lam_init)
    for h in range(H_DIFF):
        a = state[2 * h + 1]
        ot = a[:ve, :tq] / a[ve:ve + 1, :tq] - lam * (a[:ve, tq:] / a[ve:ve + 1, tq:])
        ot = ot * lax.rsqrt(jnp.mean(ot * ot, axis=0, keepdims=True) + RMS_EPS)
        o = ot.T * g_ref[...] * (1.0 - lam_init)
        o_ref[:, h * LANES:(h + 1) * LANES] = o.astype(BF16)


def _strict_upper_sum_matrix(n):
    j = lax.broadcasted_iota(jnp.int32, (n, n), 0)
    s = lax.broadcasted_iota(jnp.int32, (n, n), 1)
    return j > s


def _suffix_sums_2pass(x, mtri):
    hi = x.astype(BF16)
    lo = (x - hi.astype(F32)).astype(BF16)
    return (jnp.dot(hi, mtri, preferred_element_type=F32) + jnp.dot(lo, mtri, preferred_element_type=F32))


def _sb_attn_kernel(q_ref, k_ref, vt_ref, o_ref, *, tq):
    qi = pl.program_id(2)
    n_heads = ATT_W // HEAD_DIM
    row, col = _tile_iota(tq)
    sub = min(tq, MXU_DIM)

    def blk(kj, state, masked):
        koff = pl.multiple_of(kj * tq, tq)
        mtri_t = jnp.where(lax.broadcasted_iota(jnp.int32, (sub, sub), 1)
                           > lax.broadcasted_iota(jnp.int32, (sub, sub), 0), 1.0, 0.0).astype(BF16)
        new = []
        for p in range(n_heads // 2):
            pr = slice(p * LANES, (p + 1) * LANES)
            k = k_ref[pl.ds(koff, tq), pr]
            zt = lax.dot_general(k, _split_halves(q_ref[:, pr]), _NT, preferred_element_type=F32)
            lsz = _log_sigmoid(zt)
            lrest = lsz - zt
            if masked:
                lrest = jnp.where(row < col, lrest, 0.0)
            hi = lrest.astype(BF16)
            lo = (lrest - hi.astype(F32)).astype(BF16)
            hl = jnp.concatenate([hi, lo], axis=1)
            run, acc = state[2 * p], state[2 * p + 1]
            later = run
            parts = []
            for sb in reversed(range(tq // sub)):
                rs = slice(sb * sub, (sb + 1) * sub)
                c2 = jnp.dot(mtri_t, hl[rs], preferred_element_type=F32)
                c = c2[:, :2 * tq] + c2[:, 2 * tq:]
                parts.append(c + later)
                later = later + c[0:1, :] + lrest[sb * sub:sb * sub + 1, :]
            cs = jnp.concatenate(parts[::-1], axis=0)
            a = jnp.exp(lsz + cs)
            if masked:
                a = jnp.where(row < col, a, 0.0)
            vt = vt_ref[pr, pl.ds(koff, tq)]
            new.append(cs[0:1, :] + lrest[0:1, :])
            new.append(acc + jnp.dot(vt, a.astype(BF16), preferred_element_type=F32))
        return tuple(new)

    init = (jnp.zeros((1, 2 * tq), F32), jnp.zeros((LANES, 2 * tq), F32)) * (n_heads // 2)
    state = _run_tiles(qi, blk, init, True)
    for p in range(n_heads // 2):
        acc = state[2 * p + 1]
        _store_pair(o_ref, p, acc[:HEAD_DIM, :tq], acc[HEAD_DIM:, tq:])


def _attn_specs(t, tq, blk0):
    nq = t // tq
    q_spec = pl.BlockSpec((tq, ATT_W), lambda bi, g, qi: (bi * nq + qi, blk0 + g))
    k_spec = pl.BlockSpec((t, ATT_W), lambda bi, g, qi: (bi, blk0 + g))
    vt_spec = pl.BlockSpec((None, ATT_W, t), lambda bi, g, qi: (bi, blk0 + g, 0))
    o_spec = pl.BlockSpec((tq, ATT_W), lambda bi, g, qi: (bi * nq + qi, g))
    return nq, q_spec, k_spec, vt_spec, o_spec


_ATTN_SEM = ("parallel", "parallel", "arbitrary")


def _fox_attn(q, k, vt, ft, fc, b, t, tq, blk0):
    nq, q_spec, k_spec, vt_spec, o_spec = _attn_specs(t, tq, blk0)
    n_heads = ATT_W // HEAD_DIM
    return pl.pallas_call(
        functools.partial(_fox_attn_kernel, tq=tq),
        grid=(b, FOX_W // ATT_W, nq),
        in_specs=[q_spec, k_spec, vt_spec,
                  pl.BlockSpec((1, n_heads, t), lambda bi, g, qi: (bi, g, 0)),
                  pl.BlockSpec((1, t, n_heads), lambda bi, g, qi: (bi, 0, g))],
        out_specs=o_spec,
        out_shape=jax.ShapeDtypeStruct((b * t, FOX_W), BF16),
        compiler_params=_cparams(*_ATTN_SEM),
        name="fox_attn",
    )(q, k, vt, ft, fc)


def _diff_attn(q, k, vt, table, bias_blk, lam_p, subln_g, lam_init, b, t, tq, blk0):
    nq, q_spec, k_spec, vt_spec, o_spec = _attn_specs(t, tq, blk0)
    return pl.pallas_call(
        functools.partial(_diff_attn_kernel, tq=tq, lam_init=lam_init),
        grid=(b, DIFF_W // ATT_W, nq),
        in_specs=[pl.BlockSpec(memory_space=pltpu.SMEM), q_spec, k_spec, vt_spec,
                  pl.BlockSpec(bias_blk.shape, lambda bi, g, qi: (0, 0, 0, 0)),
                  pl.BlockSpec(lam_p.shape, lambda bi, g, qi: (0, 0)),
                  pl.BlockSpec((1, LANES), lambda bi, g, qi: (0, 0))],
        out_specs=o_spec,
        out_shape=jax.ShapeDtypeStruct((b * t, DIFF_W), BF16),
        compiler_params=_cparams(*_ATTN_SEM),
        name="diff_attn",
    )(table, q, k, vt, bias_blk, lam_p, subln_g.reshape(1, LANES))


def _sb_attn(q, k, vt, width, b, t, tq):
    nq, q_spec, k_spec, vt_spec, o_spec = _attn_specs(t, tq, 0)
    return pl.pallas_call(
        functools.partial(_sb_attn_kernel, tq=tq),
        grid=(b, width // ATT_W, nq),
        in_specs=[q_spec, k_spec, vt_spec],
        out_specs=o_spec,
        out_shape=jax.ShapeDtypeStruct((b * t, width), BF16),
        compiler_params=_cparams(*_ATTN_SEM),
        name="sb_attn",
    )(q, k, vt)


def _head_mask(n_heads, width, group):
    hrow = lax.broadcasted_iota(jnp.int32, (n_heads, width), 0)
    hcol = lax.broadcasted_iota(jnp.int32, (n_heads, width), 1) // group
    return hrow == hcol


def _build_wq(q, n_heads):
    n_q, w = q.shape
    mask = _head_mask(n_heads, w, HEAD_DIM)
    parts = [jnp.where(mask, jnp.broadcast_to(q[i:i + 1, :], (n_heads, w)), 0.0) for i in range(n_q)]
    return jnp.concatenate(parts, axis=0).astype(BF16)


def _query_of_row(n_rows, n_heads, width):
    return lax.broadcasted_iota(jnp.int32, (n_rows, width), 0) // n_heads


def _fill_new_page(new_sc, kvnew_ref):
    new_sc[...] = jnp.zeros_like(new_sc)
    new_sc[:, :, 0:kvnew_ref.shape[3]] = kvnew_ref[0]


def _dec_softmax_update(s, vts, m_sc, l_sc, acc_sc):
    m_old = m_sc[...]
    m_new = jnp.maximum(m_old, jnp.max(s, axis=-1, keepdims=True))
    alpha = jnp.exp(m_old - m_new)
    p = jnp.exp(s - m_new)
    l_sc[...] = alpha * l_sc[...] + jnp.sum(p, axis=-1, keepdims=True)
    acc_sc[...] = alpha * acc_sc[...] + lax.dot_general(
        p.astype(BF16), jnp.concatenate(vts, axis=1), _NT, preferred_element_type=F32)
    m_sc[...] = m_new


def _gather_heads(acc, n_q, n_heads, mask):
    rows = [jnp.sum(jnp.where(mask, acc[i * n_heads:(i + 1) * n_heads], 0.0), axis=0, keepdims=True)
            for i in range(n_q)]
    return jnp.concatenate(rows, axis=0)


def _fox_dec_kernel(pt_ref, q_ref, kvnew_ref, lfn_ref, *refs, n_pages):
    del pt_ref
    pages = refs[:n_pages]
    lfs = refs[n_pages:2 * n_pages]
    o_ref = refs[2 * n_pages]
    wq_sc, m_sc, l_sc, acc_sc, car_sc, fq_sc, new_sc = refs[2 * n_pages + 1:]
    p = pl.program_id(1)
    n_q, w = q_ref.shape[1], q_ref.shape[2]
    n_rows = n_q * H_FOX
    page = pages[0].shape[2]

    @pl.when(p == 0)
    def _():
        wq = _build_wq(q_ref[0], H_FOX)
        wq_sc[...] = wq
        lfn = lfn_ref[0]
        run = jnp.zeros((H_FOX, 1), F32)
        cums = []
        for i in range(n_q):
            run = run + lfn[:, i:i + 1]
            cums.append(run)
        fq = jnp.concatenate(cums, axis=0)
        fq_sc[...] = fq
        lane = lax.broadcasted_iota(jnp.int32, (H_FOX, page), 1)
        fnew = jnp.zeros((H_FOX, page), F32)
        for j in range(n_q):
            fnew = jnp.where(lane == j, cums[j], fnew)
        fnew = jnp.concatenate([fnew] * n_q, axis=0)
        _fill_new_page(new_sc, kvnew_ref)
        s = jnp.dot(wq, new_sc[0].astype(BF16), preferred_element_type=F32) + (fq - fnew)
        jj = lax.broadcasted_iota(jnp.int32, (n_rows, page), 1)
        s = jnp.where(jj <= _query_of_row(n_rows, H_FOX, page), s, NEG)
        _init_softmax_state(m_sc, l_sc, acc_sc)
        car_sc[...] = jnp.zeros_like(car_sc)
        _dec_softmax_update(s, [new_sc[1].astype(BF16)], m_sc, l_sc, acc_sc)

    wq = wq_sc[...]
    mtri = jnp.where(_strict_upper_sum_matrix(page), 1.0, 0.0).astype(F32)
    lfx = jnp.concatenate([lf[...] for lf in lfs], axis=0)
    sfx = jnp.dot(lfx, mtri, preferred_element_type=F32, precision=lax.Precision.HIGHEST)
    carry = car_sc[...]
    decay = []
    for g in range(n_pages):
        sl = slice(g * H_FOX, (g + 1) * H_FOX)
        suf = sfx[sl] + carry
        carry = carry + sfx[sl, 0:1] + lfx[sl, 0:1]
        decay.append(jnp.concatenate([suf] * n_q, axis=0))
    car_sc[...] = carry
    kts = jnp.concatenate([pg[0].astype(BF16) for pg in pages], axis=1)
    s = jnp.dot(wq, kts, preferred_element_type=F32) + jnp.concatenate(decay, axis=1) + fq_sc[...]
    _dec_softmax_update(s, [pg[1].astype(BF16) for pg in pages], m_sc, l_sc, acc_sc)

    @pl.when(p == pl.num_programs(1) - 1)
    def _():
        accn = acc_sc[...] / l_sc[...]
        o_ref[0] = _gather_heads(accn, n_q, H_FOX, _head_mask(H_FOX, w, HEAD_DIM))


def _diff_dec_kernel(pt_ref, q_ref, kvnew_ref, dec_ref, lam_ref, g_ref, *refs, n_pages, lam_init):
    del pt_ref
    pages = refs[:n_pages]
    o_ref = refs[n_pages]
    m_sc, l_sc, acc_sc, new_sc = refs[n_pages + 1:]
    p = pl.program_id(1)
    n_q = o_ref.shape[1]
    rph = 2 * n_q
    n_rows = H_DIFF * rph
    stride = 2 * H_DIFF
    page = pages[0].shape[0] // stride

    def head_rows(refs_, first, h):
        return jnp.concatenate([r[pl.ds(first + h, page, stride=stride), :].astype(BF16) for r in refs_], axis=0)

    def scores(refs_):
        parts = []
        for h in range(H_DIFF):
            qh = q_ref[0, h * rph:(h + 1) * rph, :].astype(BF16)
            parts.append(lax.dot_general(qh, head_rows(refs_, 0, h), _NT, preferred_element_type=F32))
        return jnp.concatenate(parts, axis=0)

    def update(s, refs_):
        m_old = m_sc[...]
        m_new = jnp.maximum(m_old, jnp.max(s, axis=-1, keepdims=True))
        alpha = jnp.exp(m_old - m_new)
        pr = jnp.exp(s - m_new)
        l_sc[...] = alpha * l_sc[...] + jnp.sum(pr, axis=-1, keepdims=True)
        acc = alpha * acc_sc[...]
        rows = []
        for h in range(H_DIFF):
            rs = slice(h * rph, (h + 1) * rph)
            rows.append(acc[rs] + jnp.dot(pr[rs].astype(BF16), head_rows(refs_, H_DIFF, h),
                                          preferred_element_type=F32))
        acc_sc[...] = jnp.concatenate(rows, axis=0)
        m_sc[...] = m_new

    @pl.when(p == 0)
    def _():
        new_sc[...] = jnp.zeros_like(new_sc)
        new_sc[0:kvnew_ref.shape[1], :] = kvnew_ref[0]
        s = scores([new_sc]) + dec_ref[1]
        jj = lax.broadcasted_iota(jnp.int32, (n_rows, page), 1)
        ii = lax.broadcasted_iota(jnp.int32, (n_rows, page), 0) % n_q
        s = jnp.where(jj <= ii, s, NEG)
        _init_softmax_state(m_sc, l_sc, acc_sc)
        update(s, [new_sc])

    far = dec_ref[2]
    bias = jnp.concatenate([jnp.where(p == 0, dec_ref[0], far)] + [far] * (n_pages - 1), axis=1)
    update(scores(pages) + bias, pages)

    @pl.when(p == pl.num_programs(1) - 1)
    def _():
        lam = _diff_lambda(lam_ref[...], lam_init)
        accn = acc_sc[...] / l_sc[...]
        outs = []
        for h in range(H_DIFF):
            blk = accn[h * rph:(h + 1) * rph]
            oh = blk[0:n_q] - lam * blk[n_q:rph]
            outs.append(_rmsnorm(oh, g_ref[...]) * (1.0 - lam_init))
        o_ref[0] = jnp.concatenate(outs, axis=1)


def _sb_dec_kernel(pt_ref, q_ref, kvnew_ref, *refs, n_pages, n_heads):
    del pt_ref
    pages = refs[:n_pages]
    o_ref = refs[n_pages]
    wq_sc, acc_sc, run_sc, new_sc = refs[n_pages + 1:]
    p = pl.program_id(1)
    n_q, w = q_ref.shape[1], q_ref.shape[2]
    n_rows = n_q * n_heads
    page = pages[0].shape[2]

    def absorb(wq, kts, vts, strict):
        mtri = jnp.where(_strict_upper_sum_matrix(page), 1.0, 0.0).astype(BF16)
        n = len(kts)
        z = jnp.dot(wq, jnp.concatenate(kts, axis=1), preferred_element_type=F32)
        lsz = _log_sigmoid(z)
        lrest = lsz - z
        if strict is not None:
            lrest = jnp.where(strict, lrest, 0.0)
        lrests = [lrest[:, g * page:(g + 1) * page] for g in range(n)]
        cs_all = _suffix_sums_2pass(jnp.concatenate(lrests, axis=0), mtri)
        run = run_sc[...]
        offs = []
        for g in range(n):
            cs = cs_all[g * n_rows:(g + 1) * n_rows]
            offs.append(cs + run)
            run = run + cs[:, 0:1] + lrests[g][:, 0:1]
        a = jnp.exp(lsz + jnp.concatenate(offs, axis=1))
        if strict is not None:
            a = jnp.where(strict, a, 0.0)
        run_sc[...] = run
        acc_sc[...] = acc_sc[...] + lax.dot_general(a.astype(BF16), jnp.concatenate(vts, axis=1), _NT,
                                                    preferred_element_type=F32)

    @pl.when(p == 0)
    def _():
        wq = _build_wq(q_ref[0], n_heads)
        wq_sc[...] = wq
        _fill_new_page(new_sc, kvnew_ref)
        acc_sc[...] = jnp.zeros_like(acc_sc)
        run_sc[...] = jnp.zeros_like(run_sc)
        jj = lax.broadcasted_iota(jnp.int32, (n_rows, page), 1)
        strict = jj < _query_of_row(n_rows, n_heads, page)
        absorb(wq, [new_sc[0].astype(BF16)], [new_sc[1].astype(BF16)], strict)

    absorb(wq_sc[...], [pg[0].astype(BF16) for pg in pages], [pg[1].astype(BF16) for pg in pages], None)

    @pl.when(p == pl.num_programs(1) - 1)
    def _():
        o_ref[0] = _gather_heads(acc_sc[...], n_q, n_heads, _head_mask(n_heads, w, HEAD_DIM))


def _page_specs(pool, layer, n_steps, n_pages):
    total = n_steps * n_pages
    tail = (0,) * (pool.ndim - 2)
    specs = []
    for g in range(n_pages):
        specs.append(pl.BlockSpec(
            (None, None) + pool.shape[2:],
            lambda b, p, pt, g=g: (layer, pt[b, total - 1 - (p * n_pages + g)]) + tail))
    return specs


def _per_batch_spec(shape):
    return pl.BlockSpec((1,) + shape[1:], lambda b, p, pt: (b,) + (0,) * (len(shape) - 1))


def _const_spec(shape):
    return pl.BlockSpec(shape, lambda b, p, pt: (0,) * len(shape))


def _decode_call(kernel_fn, name, page_table, pool, layer, n_pages, batch_args, const_args, extra_pools,
                 n_q, out_w, scratch):
    db, n_tbl = page_table.shape
    n_steps = n_tbl // n_pages
    in_specs = [_per_batch_spec(a.shape) for a in batch_args] + [_const_spec(a.shape) for a in const_args]
    args = list(batch_args) + list(const_args)
    for pl_arr in (pool,) + tuple(extra_pools):
        in_specs += _page_specs(pl_arr, layer, n_steps, n_pages)
        args += [pl_arr] * n_pages
    return pl.pallas_call(
        kernel_fn,
        grid_spec=pltpu.PrefetchScalarGridSpec(
            num_scalar_prefetch=1, grid=(db, n_steps), in_specs=in_specs,
            out_specs=pl.BlockSpec((1, n_q, out_w), lambda b, p, pt: (b, 0, 0)),
            scratch_shapes=scratch),
        out_shape=jax.ShapeDtypeStruct((db, n_q, out_w), F32),
        compiler_params=_cparams("parallel", "arbitrary", vmem=DECODE_VMEM_LIMIT),
        name=name,
    )(page_table, *args)


def _pages_per_step(page_table, pool):
    page_bytes = math.prod(pool.shape[2:]) * pool.dtype.itemsize
    return _pick_tile(page_table.shape[1], max(1, PAGE_STEP_BYTES // page_bytes))


def _transposed_pool(cache):
    l, n_pool, page, two, h, dd = cache.shape
    return cache.transpose(0, 1, 3, 4, 5, 2).reshape(l, n_pool, two, h * dd, page)


def _transposed_new(kv_s, db, n_q, width):
    return kv_s.reshape(db, n_q, 2, width).transpose(0, 2, 3, 1)


def _ab_layer(xp, xs, dims, page_table, cache_kv_a, cache_lf, cache_kv_b, layer, g, w_in, b_f, lam_p,
              subln_g, w_out, table, bias_blk, dec_bias, lam_init, tq):
    b, t, db, n_q = dims
    splits = [0, FOX_W, 2 * FOX_W, 3 * FOX_W, 3 * FOX_W + H_FOX, 3 * FOX_W + H_FOX + DIFF_W,
              3 * FOX_W + H_FOX + 2 * DIFF_W, 3 * FOX_W + H_FOX + 3 * DIFF_W]
    q_a, k_a, v_a, f_a, q_b, k_b, v_b = [w_in[:, splits[i]:splits[i + 1]] for i in range(7)]
    wq = jnp.concatenate([q_a, q_b], axis=1).astype(BF16)
    wkva = jnp.concatenate([k_a, v_a], axis=1).astype(BF16)
    wkvb = jnp.concatenate([k_b, v_b], axis=1).astype(BF16)
    wf = jnp.pad(f_a, ((0, 0), (0, F_PAD - H_FOX))).astype(BF16)
    bf = jnp.pad(b_f, (0, F_PAD - H_FOX)).reshape(1, F_PAD)
    wo_a = w_out[:FOX_W].astype(BF16)
    wo_b = w_out[FOX_W:].astype(BF16)
    wk = jnp.concatenate([k_a, k_b], axis=1).astype(BF16)
    q, k, vt, kvat, kvb, lf = _proj_t(xp, g, b, t, wq, wk, wkva.T,
                                      (v_b.astype(BF16), v_b.T.astype(BF16), wf, bf), H_FOX)
    ft = _cumsum_lanes(lf.reshape(b, t, H_FOX).transpose(0, 2, 1))
    fc = ft.transpose(0, 2, 1)
    o_a = _fox_attn(q, k, vt, ft, fc, b, t, tq, 0)
    o_b = _diff_attn(q, k, vt, table, bias_blk, lam_p, subln_g, lam_init, b, t, tq, FOX_W // ATT_W)
    xp = _outproj(xp, [o_a, o_b], [wo_a, wo_b])
    kva = kvat.reshape(b, 2, H_FOX, HEAD_DIM, t).transpose(0, 4, 1, 2, 3)

    qkv_s, kva_s, kvb_s, lf_s = _proj(xs, g, wq, [wkva, wkvb], wf, bf, H_FOX)
    q_s = qkv_s[:, :FOX_W + DIFF_W].astype(F32).reshape(db, n_q, FOX_W + DIFF_W)
    lfn = lf_s.reshape(db, n_q, H_FOX).transpose(0, 2, 1)
    n_pool, page = cache_kv_a.shape[1], cache_kv_a.shape[2]
    pool_a = _transposed_pool(cache_kv_a)
    pool_lf = cache_lf.transpose(0, 1, 3, 2)
    pool_b = cache_kv_b.reshape(cache_kv_b.shape[0], n_pool, page * 2 * H_DIFF, 2 * HEAD_DIM)
    n_pages = _pages_per_step(page_table, pool_a)
    rows_a = n_q * H_FOX
    o_a_s = _decode_call(
        functools.partial(_fox_dec_kernel, n_pages=n_pages), "fox_decode", page_table, pool_a, layer, n_pages,
        [q_s[:, :, :FOX_W], _transposed_new(kva_s, db, n_q, FOX_W), lfn], [], [pool_lf], n_q, FOX_W,
        [pltpu.VMEM((rows_a, FOX_W), BF16), pltpu.VMEM((rows_a, 1), F32), pltpu.VMEM((rows_a, 1), F32),
         pltpu.VMEM((rows_a, FOX_W), F32), pltpu.VMEM((H_FOX, 1), F32), pltpu.VMEM((rows_a, 1), F32),
         pltpu.VMEM((2, FOX_W, page), F32)])
    qb = q_s[:, :, FOX_W:].reshape(db, n_q, H_DIFF, 2, HEAD_DIM).transpose(0, 2, 3, 1, 4)
    q_rows = (qb[:, :, :, :, None, :] * jnp.eye(2, dtype=F32)[None, None, :, None, :, None]).reshape(
        db, H_DIFF * 2 * n_q, 2 * HEAD_DIM)
    rows_b = n_q * 2 * H_DIFF
    o_b_s = _decode_call(
        functools.partial(_diff_dec_kernel, n_pages=n_pages, lam_init=lam_init), "diff_decode", page_table,
        pool_b, layer, n_pages,
        [q_rows, kvb_s.reshape(db, n_q * 2 * H_DIFF, 2 * HEAD_DIM)],
        [dec_bias, lam_p, subln_g.reshape(1, LANES)], [], n_q, DIFF_W,
        [pltpu.VMEM((rows_b, 1), F32), pltpu.VMEM((rows_b, 1), F32), pltpu.VMEM((rows_b, 2 * HEAD_DIM), F32),
         pltpu.VMEM((page * 2 * H_DIFF, 2 * HEAD_DIM), F32)])
    xs = _outproj(xs, [o_a_s.reshape(db * n_q, FOX_W), o_b_s.reshape(db * n_q, DIFF_W)], [wo_a, wo_b])
    return xp, xs, (kva, lf, kvb), (kva_s, lf_s, kvb_s)


def _sb_layer(xp, xs, dims, page_table, cache_kv, layer, g, w_in, w_out, tq):
    b, t, db, n_q = dims
    width = w_in.shape[1] // 3
    n_heads = width // HEAD_DIM
    wq = w_in[:, :width].astype(BF16)
    wkv = w_in[:, width:].astype(BF16)
    wo = w_out.astype(BF16)

    q, k, vt, kvt = _proj_t(xp, g, b, t, wq, wkv[:, :width], wkv.T)
    o = _sb_attn(q, k, vt, width, b, t, tq)
    xp = _outproj(xp, [o], [wo])
    kv = kvt.reshape(b, 2, n_heads, HEAD_DIM, t).transpose(0, 4, 1, 2, 3)

    qkv_s, kv_s = _proj(xs, g, wq, [wkv])
    q_s = qkv_s[:, :width].astype(F32).reshape(db, n_q, width)
    page = cache_kv.shape[2]
    pool = _transposed_pool(cache_kv)
    n_pages = _pages_per_step(page_table, pool)
    rows = n_q * n_heads
    o_s = _decode_call(
        functools.partial(_sb_dec_kernel, n_pages=n_pages, n_heads=n_heads), "sb_decode", page_table, pool,
        layer, n_pages, [q_s, _transposed_new(kv_s, db, n_q, width)], [], [], n_q, width,
        [pltpu.VMEM((rows, width), BF16), pltpu.VMEM((rows, width), F32), pltpu.VMEM((rows, 1), F32),
         pltpu.VMEM((2, width, page), F32)])
    xs = _outproj(xs, [o_s.reshape(db * n_q, width)], [wo])
    return xp, xs, kv, kv_s


def kernel(x_prompt, x_sample, cache_fox_kv, cache_fox_logf, cache_diff_kv, cache_sb_kv, page_table, rel_bias_table, norm_g, final_norm_g, w_ffn_up, w_ffn_down, w_in_ab, b_forget, diff_lambda, diff_subln_g, w_out_ab, w_in_sb, w_out_sb):
    b, t, d = x_prompt.shape
    db, n_q, _ = x_sample.shape
    depth = norm_g.shape[0]
    page = cache_fox_kv.shape[2]
    assert page >= MAX_DISTANCE and page % LANES == 0 and n_q <= page
    tq = _pick_tile(t, 512)
    assert tq >= MAX_DISTANCE
    dims = (b, t, db, n_q)
    xp = x_prompt.reshape(b * t, d)
    xs = x_sample.reshape(db * n_q, d)
    bias_blk, dec_bias = _bias_tables(rel_bias_table, tq, n_q, page)

    p_fox_kv, p_fox_lf, p_diff_kv, p_sb_kv = [], [], [], []
    s_fox_kv, s_fox_lf, s_diff_kv, s_sb_kv = [], [], [], []
    for i in range(depth):
        l = i // 2
        last = i == depth - 1
        wu = w_ffn_up[i].astype(BF16)
        wd = w_ffn_down[i].astype(BF16)
        xp = _ffn(xp, norm_g[i, 0], wu[0], wd[0])
        xs = _ffn(xs, norm_g[i, 0], wu[0], wd[0])
        if i % 2 == 0:
            lam_init = 0.8 - 0.6 * math.exp(-0.3 * i)
            xp, xs, (kva, lf, kvb), (kva_s, lf_s, kvb_s) = _ab_layer(
                xp, xs, dims, page_table, cache_fox_kv, cache_fox_logf, cache_diff_kv, l, norm_g[i, 1],
                w_in_ab[l], b_forget[l], diff_lambda[l], diff_subln_g[l], w_out_ab[l], rel_bias_table,
                bias_blk, dec_bias, lam_init, tq)
            p_fox_kv.append(kva.reshape(b, t, 2, H_FOX, HEAD_DIM))
            p_fox_lf.append(lf.reshape(b, t, H_FOX))
            p_diff_kv.append(kvb.reshape(b, t, 2, H_DIFF, 2 * HEAD_DIM))
            s_fox_kv.append(kva_s.reshape(db, n_q, 2, H_FOX, HEAD_DIM))
            s_fox_lf.append(lf_s.reshape(db, n_q, H_FOX))
            s_diff_kv.append(kvb_s.reshape(db, n_q, 2, H_DIFF, 2 * HEAD_DIM))
        else:
            xp, xs, kv, kv_s = _sb_layer(xp, xs, dims, page_table, cache_sb_kv, l, norm_g[i, 1],
                                         w_in_sb[l], w_out_sb[l], tq)
            n_heads = w_in_sb.shape[-1] // (3 * HEAD_DIM)
            p_sb_kv.append(kv.reshape(b, t, 2, n_heads, HEAD_DIM))
            s_sb_kv.append(kv_s.reshape(db, n_q, 2, n_heads, HEAD_DIM))
        fg = final_norm_g if last else None
        xp = _ffn(xp, norm_g[i, 2], wu[1], wd[1], final_g=fg)
        xs = _ffn(xs, norm_g[i, 2], wu[1], wd[1], final_g=fg)
    return (xp.reshape(b, t, d), xs.reshape(db, n_q, d),
            jnp.stack(p_fox_kv), jnp.stack(p_fox_lf), jnp.stack(p_diff_kv), jnp.stack(p_sb_kv),
            jnp.stack(s_fox_kv), jnp.stack(s_fox_lf), jnp.stack(s_diff_kv), jnp.stack(s_sb_kv))
```

```python
import functools
import math

import jax
import jax.numpy as jnp
from jax import lax
from jax.experimental import pallas as pl
from jax.experimental.pallas import tpu as pltpu

F32 = jnp.float32
BF16 = jnp.bfloat16

HEAD_DIM = 64
H_FOX = 8
H_DIFF = 4
FOX_W = H_FOX * HEAD_DIM
DIFF_W = H_DIFF * 2 * HEAD_DIM
N_BUCKETS = 32
MAX_EXACT = N_BUCKETS // 2
MAX_DISTANCE = 128
RMS_EPS = 1e-5
NEG = -1e30
Q_SCALE = HEAD_DIM ** -0.5

LANES = 128
MXU_DIM = 256
F_PAD = LANES
ATT_W = 4 * LANES
ONES_ROWS = 16
VMEM_LIMIT = 48 * 1024 * 1024
PAGE_STEP_BYTES = 16 * 1024 * 1024
DECODE_VMEM_LIMIT = 2 * PAGE_STEP_BYTES + 24 * 1024 * 1024

_NT = (((1,), (1,)), ((), ()))


def _cparams(*sem, vmem=VMEM_LIMIT):
    return pltpu.CompilerParams(dimension_semantics=sem, vmem_limit_bytes=vmem)


def _rmsnorm(x, g):
    return x * lax.rsqrt(jnp.mean(x * x, axis=-1, keepdims=True) + RMS_EPS) * g


def _log_sigmoid(x):
    return jnp.minimum(x, 0.0) - jnp.log(1.0 + jnp.exp(-jnp.abs(x)))


def _pick_tile(n, target):
    t = min(n, target)
    while n % t:
        t //= 2
    return t


def _ffn_kernel(x_ref, g_ref, wup_ref, wd_ref, *rest, final, tf, n_pre):
    pre_o, pre_w, rest = rest[:n_pre], rest[n_pre:2 * n_pre], rest[2 * n_pre:]
    if final:
        fg_ref, o_ref = rest
    else:
        (o_ref,) = rest
    d_ff = wd_ref.shape[0]
    x = x_ref[...]
    for po_ref, pw_ref in zip(pre_o, pre_w):
        x = x + jnp.dot(po_ref[...].astype(BF16), pw_ref[...], preferred_element_type=F32)
    h = _rmsnorm(x, g_ref[...]).astype(BF16)
    y = x
    for c in range(d_ff // tf):
        gate = jnp.dot(h, wup_ref[:, c * tf:(c + 1) * tf], preferred_element_type=F32)
        up = jnp.dot(h, wup_ref[:, d_ff + c * tf:d_ff + (c + 1) * tf], preferred_element_type=F32)
        act = (gate * jax.nn.sigmoid(gate) * up).astype(BF16)
        y = y + 0.5 * jnp.dot(act, wd_ref[c * tf:(c + 1) * tf, :], preferred_element_type=F32)
    if final:
        y = _rmsnorm(y, fg_ref[...])
    o_ref[...] = y


def _ffn(x, g, w_up, w_down, final_g=None, pre=None, tm_target=512, tf_target=1408):
    pre_parts, pre_ws = pre if pre is not None else ((), ())
    m, d = x.shape
    d_ff = w_down.shape[0]
    tm = _pick_tile(m, tm_target)
    tf = _pick_tile(d_ff, tf_target)
    final = final_g is not None
    const = lambda i: (0, 0)
    resident = dict(pipeline_mode=pl.Buffered(1))
    in_specs = [
        pl.BlockSpec((tm, d), lambda i: (i, 0)),
        pl.BlockSpec((1, d), const),
        pl.BlockSpec(w_up.shape, const, **resident),
        pl.BlockSpec(w_down.shape, const, **resident),
    ]
    args = [x, g.reshape(1, d), w_up, w_down]
    in_specs += [pl.BlockSpec((tm, p.shape[1]), lambda i: (i, 0)) for p in pre_parts]
    in_specs += [pl.BlockSpec(w.shape, const, **resident) for w in pre_ws]
    args += [*pre_parts, *pre_ws]
    if final:
        in_specs.append(pl.BlockSpec((1, d), const))
        args.append(final_g.reshape(1, d))
    return pl.pallas_call(
        functools.partial(_ffn_kernel, final=final, tf=tf, n_pre=len(pre_parts)),
        grid=(m // tm,),
        in_specs=in_specs,
        out_specs=pl.BlockSpec((tm, d), lambda i: (i, 0)),
        out_shape=jax.ShapeDtypeStruct((m, d), F32),
        compiler_params=_cparams("parallel"),
        name="half_ffn",
    )(*args)


def _proj_kernel(x_ref, g_ref, wq_ref, *rest, n_kv, with_f):
    kv_w = rest[:n_kv]
    rest = rest[n_kv:]
    if with_f:
        wf_ref, bf_ref = rest[:2]
        rest = rest[2:]
    qkv_ref = rest[0]
    kv_o = rest[1:1 + n_kv]
    h = _rmsnorm(x_ref[...], g_ref[...]).astype(BF16)
    q = jnp.dot(h, wq_ref[...], preferred_element_type=F32) * Q_SCALE
    wq_cols = wq_ref.shape[1]
    qkv_ref[:, :wq_cols] = q.astype(BF16)
    off = wq_cols
    for w_ref, o_ref in zip(kv_w, kv_o):
        kv = jnp.dot(h, w_ref[...], preferred_element_type=F32)
        o_ref[...] = kv
        qkv_ref[:, off:off + kv.shape[1]] = kv.astype(BF16)
        off += kv.shape[1]
    if with_f:
        lf_ref = rest[1 + n_kv]
        f = jnp.dot(h, wf_ref[...], preferred_element_type=F32) + bf_ref[...]
        lf_ref[...] = _log_sigmoid(f)[:, :lf_ref.shape[1]]


def _proj(x, g, wq, kv_ws, wf=None, bf=None, n_f=0, tm_target=512):
    m, d = x.shape
    tm = _pick_tile(m, tm_target)
    with_f = wf is not None
    n_kv = len(kv_ws)
    tot = wq.shape[1] + sum(w.shape[1] for w in kv_ws)
    const = lambda i: (0, 0)
    row = lambda i: (i, 0)
    in_specs = [pl.BlockSpec((tm, d), row), pl.BlockSpec((1, d), const),
                pl.BlockSpec(wq.shape, const)]
    in_specs += [pl.BlockSpec(w.shape, const) for w in kv_ws]
    args = [x, g.reshape(1, d), wq, *kv_ws]
    out_shape = [jax.ShapeDtypeStruct((m, tot), BF16)]
    out_specs = [pl.BlockSpec((tm, tot), row)]
    for w in kv_ws:
        out_shape.append(jax.ShapeDtypeStruct((m, w.shape[1]), F32))
        out_specs.append(pl.BlockSpec((tm, w.shape[1]), row))
    if with_f:
        in_specs += [pl.BlockSpec(wf.shape, const), pl.BlockSpec(bf.shape, const)]
        args += [wf, bf]
        out_shape.append(jax.ShapeDtypeStruct((m, n_f), F32))
        out_specs.append(pl.BlockSpec((tm, n_f), row))
    return pl.pallas_call(
        functools.partial(_proj_kernel, n_kv=n_kv, with_f=with_f),
        grid=(m // tm,),
        in_specs=in_specs,
        out_specs=out_specs,
        out_shape=out_shape,
        compiler_params=_cparams("parallel"),
        name="in_proj",
    )(*args)


def _proj_t_kernel(x_ref, g_ref, wq_ref, wk_ref, wkvt_ref, *rest, ab):
    if ab:
        wvb_ref, wvbt_ref, wf_ref, bf_ref, q_o, k_o, vt_o, kvt_o, kvb_o, lf_o = rest
    else:
        q_o, k_o, vt_o, kvt_o = rest
    h = _rmsnorm(x_ref[...], g_ref[...]).astype(BF16)
    q_o[...] = (jnp.dot(h, wq_ref[...], preferred_element_type=F32) * Q_SCALE).astype(BF16)
    k = jnp.dot(h, wk_ref[...], preferred_element_type=F32)
    k_o[...] = k.astype(BF16)
    kvt = lax.dot_general(wkvt_ref[...], h, _NT, preferred_element_type=F32)
    kvt_o[0] = kvt
    w = kvt.shape[0] // 2
    vt_o[0, 0:w, :] = kvt[w:].astype(BF16)
    if ab:
        wb = wvb_ref.shape[1]
        vb = jnp.dot(h, wvb_ref[...], preferred_element_type=F32)
        nh = wb // LANES
        for j in range(nh):
            kvb_o[:, j, :] = k[:, k.shape[1] - wb + j * LANES:k.shape[1] - wb + (j + 1) * LANES]
            kvb_o[:, nh + j, :] = vb[:, j * LANES:(j + 1) * LANES]
        vt_o[0, w:, :] = lax.dot_general(wvbt_ref[...], h, _NT, preferred_element_type=F32).astype(BF16)
        f = jnp.dot(h, wf_ref[...], preferred_element_type=F32) + bf_ref[...]
        lf_o[...] = _log_sigmoid(f)[:, :lf_o.shape[1]]


def _proj_t(x, g, b, t, wq, wk, wkvt, ab_ws=None, n_f=0, tm_target=256):
    m, d = x.shape
    tm = _pick_tile(t, tm_target)
    nt = t // tm
    ab = ab_ws is not None
    const = lambda bi, ti: (0, 0)
    row = lambda bi, ti: (bi * nt + ti, 0)
    seq = lambda bi, ti: (bi, 0, ti)
    in_specs = [pl.BlockSpec((tm, d), row), pl.BlockSpec((1, d), const), pl.BlockSpec(wq.shape, const),
                pl.BlockSpec(wk.shape, const), pl.BlockSpec(wkvt.shape, const)]
    args = [x, g.reshape(1, d), wq, wk, wkvt]
    w2 = wkvt.shape[0]
    vt_rows = w2 // 2 + (ab_ws[1].shape[0] if ab else 0)
    out_shape = [jax.ShapeDtypeStruct((m, wq.shape[1]), BF16), jax.ShapeDtypeStruct((m, wk.shape[1]), BF16),
                 jax.ShapeDtypeStruct((b, vt_rows, t), BF16), jax.ShapeDtypeStruct((b, w2, t), F32)]
    out_specs = [pl.BlockSpec((tm, wq.shape[1]), row), pl.BlockSpec((tm, wk.shape[1]), row),
                 pl.BlockSpec((1, vt_rows, tm), seq), pl.BlockSpec((1, w2, tm), seq)]
    if ab:
        wvb, wvbt, wf, bf = ab_ws
        in_specs += [pl.BlockSpec(a.shape, const) for a in ab_ws]
        args += list(ab_ws)
        kvb_rows = 2 * wvb.shape[1] // LANES
        out_shape += [jax.ShapeDtypeStruct((m, kvb_rows, LANES), F32), jax.ShapeDtypeStruct((m, n_f), F32)]
        out_specs += [pl.BlockSpec((tm, kvb_rows, LANES), lambda bi, ti: (bi * nt + ti, 0, 0)),
                      pl.BlockSpec((tm, n_f), row)]
    return pl.pallas_call(
        functools.partial(_proj_t_kernel, ab=ab),
        grid=(b, nt),
        in_specs=in_specs,
        out_specs=out_specs,
        out_shape=out_shape,
        compiler_params=_cparams("parallel", "parallel"),
        name="in_proj_t",
    )(*args)


def _cumsum_lanes_kernel(x_ref, o_ref):
    x = x_ref[0]
    t = x.shape[1]
    lane = lax.broadcasted_iota(jnp.int32, x.shape, 1)
    d = 1
    while d < t:
        x = x + jnp.where(lane >= d, pltpu.roll(x, d, axis=1), 0.0)
        d *= 2
    o_ref[0] = x


def _cumsum_lanes(x):
    b, h, t = x.shape
    spec = pl.BlockSpec((1, h, t), lambda i: (i, 0, 0))
    return pl.pallas_call(
        _cumsum_lanes_kernel, grid=(b,), in_specs=[spec], out_specs=spec,
        out_shape=jax.ShapeDtypeStruct(x.shape, F32),
        compiler_params=_cparams("parallel"), name="logf_cumsum",
    )(x)


def _bucket(n):
    nf = jnp.maximum(n, MAX_EXACT).astype(F32)
    large = MAX_EXACT + (jnp.log(nf / MAX_EXACT) / math.log(MAX_DISTANCE / MAX_EXACT)
                         * (N_BUCKETS - MAX_EXACT)).astype(jnp.int32)
    return jnp.where(n < MAX_EXACT, n, jnp.minimum(large, N_BUCKETS - 1))


def _bias_kernel(tbl_ref, blk_ref, dec_ref, *, tq, n_q, page):
    h = pl.program_id(0)
    r = lax.broadcasted_iota(jnp.int32, (tq, tq), 0)
    c = lax.broadcasted_iota(jnp.int32, (tq, tq), 1)
    for which, base in enumerate((0, tq)):
        bucket = _bucket(jnp.maximum(c - r + base, 0))
        val = jnp.zeros((tq, tq), F32)
        for b in range(N_BUCKETS):
            val = jnp.where(bucket == b, tbl_ref[b, h], val)
        blk_ref[0, which] = val
    n_rows = n_q * 2 * H_DIFF
    rr = lax.broadcasted_iota(jnp.int32, (n_rows, page), 0)
    jj = lax.broadcasted_iota(jnp.int32, (n_rows, page), 1)
    qi = rr % n_q
    hd = rr // (2 * n_q)
    dists = (page + qi - jj, jnp.maximum(qi - jj, 0), jnp.full((n_rows, page), MAX_DISTANCE, jnp.int32))
    for which, dist in enumerate(dists):
        bucket = _bucket(dist)
        val = jnp.zeros((n_rows, page), F32)
        for b in range(N_BUCKETS):
            tb = jnp.zeros((n_rows, page), F32)
            for hh in range(H_DIFF):
                tb = jnp.where(hd == hh, tbl_ref[b, hh], tb)
            val = jnp.where(bucket == b, tb, val)
        dec_ref[which] = val


def _bias_tables(table, tq, n_q, page):
    n_rows = n_q * 2 * H_DIFF
    return pl.pallas_call(
        functools.partial(_bias_kernel, tq=tq, n_q=n_q, page=page),
        grid=(H_DIFF,),
        in_specs=[pl.BlockSpec(memory_space=pltpu.SMEM)],
        out_specs=[pl.BlockSpec((1, 2, tq, tq), lambda h: (h, 0, 0, 0)),
                   pl.BlockSpec((3, n_rows, page), lambda h: (0, 0, 0))],
        out_shape=[jax.ShapeDtypeStruct((H_DIFF, 2, tq, tq), F32),
                   jax.ShapeDtypeStruct((3, n_rows, page), F32)],
        compiler_params=_cparams("arbitrary"),
        name="rel_bias_tables",
    )(table)


def _half_mask(tq, upper):
    lane = lax.broadcasted_iota(jnp.int32, (tq, LANES), 1)
    return lane >= HEAD_DIM if upper else lane < HEAD_DIM


def _tile_iota(tq):
    row = lax.broadcasted_iota(jnp.int32, (tq, 2 * tq), 0)
    col = lax.broadcasted_iota(jnp.int32, (tq, 2 * tq), 1)
    return row, jnp.where(col >= tq, col - tq, col)


def _split_halves(q):
    zero = jnp.zeros_like(q)
    return jnp.concatenate([jnp.where(_half_mask(q.shape[0], False), q, zero),
                            jnp.where(_half_mask(q.shape[0], True), q, zero)], axis=0)


def _with_ones_rows(vt):
    return jnp.concatenate([vt, jnp.ones((ONES_ROWS, vt.shape[1]), vt.dtype)], axis=0)


def _softmax_chain_step(st, vt, m_old, acc):
    m_new = jnp.maximum(m_old, jnp.max(st, axis=0, keepdims=True))
    alpha = jnp.exp(m_old - m_new)
    pt = jnp.exp(st - m_new).astype(BF16)
    return m_new, alpha * acc + jnp.dot(vt, pt, preferred_element_type=F32)


def _run_tiles(qi, blk, state, diag_first):
    if diag_first:
        state = blk(qi, state, True)
        return lax.fori_loop(0, qi, lambda t, s: blk(qi - 1 - t, s, False), state)
    state = lax.fori_loop(0, qi, lambda kj, s: blk(kj, s, False), state)
    return blk(qi, state, True)


def _init_softmax_state(m_sc, l_sc, acc_sc):
    m_sc[...] = jnp.full_like(m_sc, NEG)
    l_sc[...] = jnp.zeros_like(l_sc)
    acc_sc[...] = jnp.zeros_like(acc_sc)


def _store_pair(o_ref, p, top, bottom):
    pair_t = jnp.concatenate([top, bottom], axis=0)
    o_ref[:, p * LANES:(p + 1) * LANES] = pair_t.T.astype(BF16)


def _fox_attn_kernel(q_ref, k_ref, vt_ref, ft_ref, fc_ref, o_ref, *, tq):
    qi = pl.program_id(2)
    n_heads = ATT_W // HEAD_DIM
    row, col = _tile_iota(tq)
    first = lax.broadcasted_iota(jnp.int32, (tq, 2 * tq), 1) < tq
    qoff = pl.multiple_of(qi * tq, tq)

    def blk(kj, state, masked):
        koff = pl.multiple_of(kj * tq, tq)
        new = []
        for p in range(n_heads // 2):
            pr = slice(p * LANES, (p + 1) * LANES)
            k = k_ref[pl.ds(koff, tq), pr]
            st = lax.dot_general(k, _split_halves(q_ref[:, pr]), _NT, preferred_element_type=F32)
            fq = jnp.concatenate([ft_ref[0, h:h + 1, pl.ds(qoff, tq)] for h in (2 * p, 2 * p + 1)], axis=1)
            fk = jnp.where(first, fc_ref[0, pl.ds(koff, tq), 2 * p:2 * p + 1],
                           fc_ref[0, pl.ds(koff, tq), 2 * p + 1:2 * p + 2])
            st = st + (fq - fk)
            if masked:
                st = jnp.where(row <= col, st, NEG)
            vt = _with_ones_rows(vt_ref[pr, pl.ds(koff, tq)])
            new.extend(_softmax_chain_step(st, vt, state[2 * p], state[2 * p + 1]))
        return tuple(new)

    init = (jnp.full((1, 2 * tq), NEG, F32), jnp.zeros((LANES + ONES_ROWS, 2 * tq), F32)) * (n_heads // 2)
    state = _run_tiles(qi, blk, init, False)
    for p in range(n_heads // 2):
        a = state[2 * p + 1]
        _store_pair(o_ref, p, a[:HEAD_DIM, :tq] / a[LANES:LANES + 1, :tq],
                    a[HEAD_DIM:LANES, tq:] / a[LANES:LANES + 1, tq:])


def _diff_lambda(lp, lam_init):
    return (jnp.exp(jnp.sum(lp[0:1] * lp[1:2], axis=-1, keepdims=True))
            - jnp.exp(jnp.sum(lp[2:3] * lp[3:4], axis=-1, keepdims=True)) + lam_init)


def _diff_attn_kernel(tbl_ref, q_ref, k_ref, vt_ref, bias_ref, lam_ref, g_ref, o_ref, *, tq, lam_init):
    qi = pl.program_id(2)
    row, col = _tile_iota(tq)
    ve = 2 * HEAD_DIM

    def blk(kj, state, masked):
        koff = pl.multiple_of(kj * tq, tq)
        new = []
        for h in range(H_DIFF):
            hs = slice(h * LANES, (h + 1) * LANES)
            q = q_ref[:, hs]
            k = k_ref[pl.ds(koff, tq), hs]
            vt = _with_ones_rows(vt_ref[h * ve:(h + 1) * ve, pl.ds(koff, tq)])
            if masked:
                bias = bias_ref[h, 0]
            else:
                bias = jnp.where(kj == qi - 1, bias_ref[h, 1], tbl_ref[N_BUCKETS - 1, h])
            st = lax.dot_general(k, _split_halves(q), _NT, preferred_element_type=F32)
            st = st + jnp.concatenate([bias, bias], axis=1)
            if masked:
                st = jnp.where(row <= col, st, NEG)
            new.extend(_softmax_chain_step(st, vt, state[2 * h], state[2 * h + 1]))
        return tuple(new)

    init = (jnp.full((1, 2 * tq), NEG, F32), jnp.zeros((ve + ONES_ROWS, 2 * tq), F32)) * H_DIFF
    state = _run_tiles(qi, blk, init, False)
    lam = _diff_lambda(lam_ref[...], lam_init)
    for h in range(H_DIFF):
        a = state[2 * h + 1]
        ot = a[:ve, :tq] / a[ve:ve + 1, :tq] - lam * (a[:ve, tq:] / a[ve:ve + 1, tq:])
        ot = ot * lax.rsqrt(jnp.mean(ot * ot, axis=0, keepdims=True) + RMS_EPS)
        o = ot.T * g_ref[...] * (1.0 - lam_init)
        o_ref[:, h * LANES:(h + 1) * LANES] = o.astype(BF16)


def _strict_upper_sum_matrix(n):
    j = lax.broadcasted_iota(jnp.int32, (n, n), 0)
    s = lax.broadcasted_iota(jnp.int32, (n, n), 1)
    return j > s


def _suffix_sums_2pass(x, mtri):
    hi = x.astype(BF16)
    lo = (x - hi.astype(F32)).astype(BF16)
    return (jnp.dot(hi, mtri, preferred_element_type=F32) + jnp.dot(lo, mtri, preferred_element_type=F32))


def _sb_attn_kernel(q_ref, k_ref, vt_ref, o_ref, *, tq):
    qi = pl.program_id(2)
    n_heads = ATT_W // HEAD_DIM
    row, col = _tile_iota(tq)
    sub = min(tq, MXU_DIM)

    def blk(kj, state, masked):
        koff = pl.multiple_of(kj * tq, tq)
        mtri_t = jnp.where(lax.broadcasted_iota(jnp.int32, (sub, sub), 1)
                           > lax.broadcasted_iota(jnp.int32, (sub, sub), 0), 1.0, 0.0).astype(BF16)
        new = []
        for p in range(n_heads // 2):
            pr = slice(p * LANES, (p + 1) * LANES)
            k = k_ref[pl.ds(koff, tq), pr]
            zt = lax.dot_general(k, _split_halves(q_ref[:, pr]), _NT, preferred_element_type=F32)
            lsz = _log_sigmoid(zt)
            lrest = lsz - zt
            if masked:
                lrest = jnp.where(row < col, lrest, 0.0)
            hi = lrest.astype(BF16)
            lo = (lrest - hi.astype(F32)).astype(BF16)
            hl = jnp.concatenate([hi, lo], axis=1)
            run, acc = state[2 * p], state[2 * p + 1]
            later = run
            parts = []
            for sb in reversed(range(tq // sub)):
                rs = slice(sb * sub, (sb + 1) * sub)
                c2 = jnp.dot(mtri_t, hl[rs], preferred_element_type=F32)
                c = c2[:, :2 * tq] + c2[:, 2 * tq:]
                parts.append(c + later)
                later = later + c[0:1, :] + lrest[sb * sub:sb * sub + 1, :]
            cs = jnp.concatenate(parts[::-1], axis=0)
            a = jnp.exp(lsz + cs)
            if masked:
                a = jnp.where(row < col, a, 0.0)
            vt = vt_ref[pr, pl.ds(koff, tq)]
            new.append(cs[0:1, :] + lrest[0:1, :])
            new.append(acc + jnp.dot(vt, a.astype(BF16), preferred_element_type=F32))
        return tuple(new)

    init = (jnp.zeros((1, 2 * tq), F32), jnp.zeros((LANES, 2 * tq), F32)) * (n_heads // 2)
    state = _run_tiles(qi, blk, init, True)
    for p in range(n_heads // 2):
        acc = state[2 * p + 1]
        _store_pair(o_ref, p, acc[:HEAD_DIM, :tq], acc[HEAD_DIM:, tq:])


def _attn_specs(t, tq, blk0):
    nq = t // tq
    q_spec = pl.BlockSpec((tq, ATT_W), lambda bi, g, qi: (bi * nq + qi, blk0 + g))
    k_spec = pl.BlockSpec((t, ATT_W), lambda bi, g, qi: (bi, blk0 + g))
    vt_spec = pl.BlockSpec((None, ATT_W, t), lambda bi, g, qi: (bi, blk0 + g, 0))
    o_spec = pl.BlockSpec((tq, ATT_W), lambda bi, g, qi: (bi * nq + qi, g))
    return nq, q_spec, k_spec, vt_spec, o_spec


_ATTN_SEM = ("parallel", "parallel", "arbitrary")


def _fox_attn(q, k, vt, ft, fc, b, t, tq, blk0):
    nq, q_spec, k_spec, vt_spec, o_spec = _attn_specs(t, tq, blk0)
    n_heads = ATT_W // HEAD_DIM
    return pl.pallas_call(
        functools.partial(_fox_attn_kernel, tq=tq),
        grid=(b, FOX_W // ATT_W, nq),
        in_specs=[q_spec, k_spec, vt_spec,
                  pl.BlockSpec((1, n_heads, t), lambda bi, g, qi: (bi, g, 0)),
                  pl.BlockSpec((1, t, n_heads), lambda bi, g, qi: (bi, 0, g))],
        out_specs=o_spec,
        out_shape=jax.ShapeDtypeStruct((b * t, FOX_W), BF16),
        compiler_params=_cparams(*_ATTN_SEM),
        name="fox_attn",
    )(q, k, vt, ft, fc)


def _diff_attn(q, k, vt, table, bias_blk, lam_p, subln_g, lam_init, b, t, tq, blk0):
    nq, q_spec, k_spec, vt_spec, o_spec = _attn_specs(t, tq, blk0)
    return pl.pallas_call(
        functools.partial(_diff_attn_kernel, tq=tq, lam_init=lam_init),
        grid=(b, DIFF_W // ATT_W, nq),
        in_specs=[pl.BlockSpec(memory_space=pltpu.SMEM), q_spec, k_spec, vt_spec,
                  pl.BlockSpec(bias_blk.shape, lambda bi, g, qi: (0, 0, 0, 0)),
                  pl.BlockSpec(lam_p.shape, lambda bi, g, qi: (0, 0)),
                  pl.BlockSpec((1, LANES), lambda bi, g, qi: (0, 0))],
        out_specs=o_spec,
        out_shape=jax.ShapeDtypeStruct((b * t, DIFF_W), BF16),
        compiler_params=_cparams(*_ATTN_SEM),
        name="diff_attn",
    )(table, q, k, vt, bias_blk, lam_p, subln_g.reshape(1, LANES))


def _sb_attn(q, k, vt, width, b, t, tq):
    nq, q_spec, k_spec, vt_spec, o_spec = _attn_specs(t, tq, 0)
    return pl.pallas_call(
        functools.partial(_sb_attn_kernel, tq=tq),
        grid=(b, width // ATT_W, nq),
        in_specs=[q_spec, k_spec, vt_spec],
        out_specs=o_spec,
        out_shape=jax.ShapeDtypeStruct((b * t, width), BF16),
        compiler_params=_cparams(*_ATTN_SEM),
        name="sb_attn",
    )(q, k, vt)


def _head_mask(n_heads, width, group):
    hrow = lax.broadcasted_iota(jnp.int32, (n_heads, width), 0)
    hcol = lax.broadcasted_iota(jnp.int32, (n_heads, width), 1) // group
    return hrow == hcol


def _build_wq(q, n_heads):
    n_q, w = q.shape
    mask = _head_mask(n_heads, w, HEAD_DIM)
    parts = [jnp.where(mask, jnp.broadcast_to(q[i:i + 1, :], (n_heads, w)), 0.0) for i in range(n_q)]
    return jnp.concatenate(parts, axis=0).astype(BF16)


def _query_of_row(n_rows, n_heads, width):
    return lax.broadcasted_iota(jnp.int32, (n_rows, width), 0) // n_heads


def _fill_new_page(new_sc, kvnew_ref):
    new_sc[...] = jnp.zeros_like(new_sc)
    new_sc[:, :, 0:kvnew_ref.shape[3]] = kvnew_ref[0]


def _dec_softmax_update(s, vts, m_sc, l_sc, acc_sc):
    m_old = m_sc[...]
    m_new = jnp.maximum(m_old, jnp.max(s, axis=-1, keepdims=True))
    alpha = jnp.exp(m_old - m_new)
    p = jnp.exp(s - m_new)
    l_sc[...] = alpha * l_sc[...] + jnp.sum(p, axis=-1, keepdims=True)
    acc_sc[...] = alpha * acc_sc[...] + lax.dot_general(
        p.astype(BF16), jnp.concatenate(vts, axis=1), _NT, preferred_element_type=F32)
    m_sc[...] = m_new


def _gather_heads(acc, n_q, n_heads, mask):
    rows = [jnp.sum(jnp.where(mask, acc[i * n_heads:(i + 1) * n_heads], 0.0), axis=0, keepdims=True)
            for i in range(n_q)]
    return jnp.concatenate(rows, axis=0)


def _fox_dec_kernel(pt_ref, q_ref, kvnew_ref, lfn_ref, *refs, n_pages):
    del pt_ref
    pages = refs[:n_pages]
    lfs = refs[n_pages:2 * n_pages]
    o_ref = refs[2 * n_pages]
    wq_sc, m_sc, l_sc, acc_sc, car_sc, fq_sc, new_sc = refs[2 * n_pages + 1:]
    p = pl.program_id(1)
    n_q, w = q_ref.shape[1], q_ref.shape[2]
    n_rows = n_q * H_FOX
    page = pages[0].shape[2]

    @pl.when(p == 0)
    def _():
        wq = _build_wq(q_ref[0], H_FOX)
        wq_sc[...] = wq
        lfn = lfn_ref[0]
        run = jnp.zeros((H_FOX, 1), F32)
        cums = []
        for i in range(n_q):
            run = run + lfn[:, i:i + 1]
            cums.append(run)
        fq = jnp.concatenate(cums, axis=0)
        fq_sc[...] = fq
        lane = lax.broadcasted_iota(jnp.int32, (H_FOX, page), 1)
        fnew = jnp.zeros((H_FOX, page), F32)
        for j in range(n_q):
            fnew = jnp.where(lane == j, cums[j], fnew)
        fnew = jnp.concatenate([fnew] * n_q, axis=0)
        _fill_new_page(new_sc, kvnew_ref)
        s = jnp.dot(wq, new_sc[0].astype(BF16), preferred_element_type=F32) + (fq - fnew)
        jj = lax.broadcasted_iota(jnp.int32, (n_rows, page), 1)
        s = jnp.where(jj <= _query_of_row(n_rows, H_FOX, page), s, NEG)
        _init_softmax_state(m_sc, l_sc, acc_sc)
        car_sc[...] = jnp.zeros_like(car_sc)
        _dec_softmax_update(s, [new_sc[1].astype(BF16)], m_sc, l_sc, acc_sc)

    wq = wq_sc[...]
    mtri = jnp.where(_strict_upper_sum_matrix(page), 1.0, 0.0).astype(F32)
    lfx = jnp.concatenate([lf[...] for lf in lfs], axis=0)
    sfx = jnp.dot(lfx, mtri, preferred_element_type=F32, precision=lax.Precision.HIGHEST)
    carry = car_sc[...]
    decay = []
    for g in range(n_pages):
        sl = slice(g * H_FOX, (g + 1) * H_FOX)
        suf = sfx[sl] + carry
        carry = carry + sfx[sl, 0:1] + lfx[sl, 0:1]
        decay.append(jnp.concatenate([suf] * n_q, axis=0))
    car_sc[...] = carry
    kts = jnp.concatenate([pg[0].astype(BF16) for pg in pages], axis=1)
    s = jnp.dot(wq, kts, preferred_element_type=F32) + jnp.concatenate(decay, axis=1) + fq_sc[...]
    _dec_softmax_update(s, [pg[1].astype(BF16) for pg in pages], m_sc, l_sc, acc_sc)

    @pl.when(p == pl.num_programs(1) - 1)
    def _():
        accn = acc_sc[...] / l_sc[...]
        o_ref[0] = _gather_heads(accn, n_q, H_FOX, _head_mask(H_FOX, w, HEAD_DIM))


def _diff_dec_kernel(pt_ref, q_ref, kvnew_ref, dec_ref, lam_ref, g_ref, *refs, n_pages, lam_init):
    del pt_ref
    pages = refs[:n_pages]
    o_ref = refs[n_pages]
    m_sc, l_sc, acc_sc, new_sc = refs[n_pages + 1:]
    p = pl.program_id(1)
    n_q = o_ref.shape[1]
    rph = 2 * n_q
    n_rows = H_DIFF * rph
    stride = 2 * H_DIFF
    page = pages[0].shape[0] // stride

    def head_rows(refs_, first, h):
        return jnp.concatenate([r[pl.ds(first + h, page, stride=stride), :].astype(BF16) for r in refs_], axis=0)

    def scores(refs_):
        parts = []
        for h in range(H_DIFF):
            qh = q_ref[0, h * rph:(h + 1) * rph, :].astype(BF16)
            parts.append(lax.dot_general(qh, head_rows(refs_, 0, h), _NT, preferred_element_type=F32))
        return jnp.concatenate(parts, axis=0)

    def update(s, refs_):
        m_old = m_sc[...]
        m_new = jnp.maximum(m_old, jnp.max(s, axis=-1, keepdims=True))
        alpha = jnp.exp(m_old - m_new)
        pr = jnp.exp(s - m_new)
        l_sc[...] = alpha * l_sc[...] + jnp.sum(pr, axis=-1, keepdims=True)
        acc = alpha * acc_sc[...]
        rows = []
        for h in range(H_DIFF):
            rs = slice(h * rph, (h + 1) * rph)
            rows.append(acc[rs] + jnp.dot(pr[rs].astype(BF16), head_rows(refs_, H_DIFF, h),
                                          preferred_element_type=F32))
        acc_sc[...] = jnp.concatenate(rows, axis=0)
        m_sc[...] = m_new

    @pl.when(p == 0)
    def _():
        new_sc[...] = jnp.zeros_like(new_sc)
        new_sc[0:kvnew_ref.shape[1], :] = kvnew_ref[0]
        s = scores([new_sc]) + dec_ref[1]
        jj = lax.broadcasted_iota(jnp.int32, (n_rows, page), 1)
        ii = lax.broadcasted_iota(jnp.int32, (n_rows, page), 0) % n_q
        s = jnp.where(jj <= ii, s, NEG)
        _init_softmax_state(m_sc, l_sc, acc_sc)
        update(s, [new_sc])

    far = dec_ref[2]
    bias = jnp.concatenate([jnp.where(p == 0, dec_ref[0], far)] + [far] * (n_pages - 1), axis=1)
    update(scores(pages) + bias, pages)

    @pl.when(p == pl.num_programs(1) - 1)
    def _():
        lam = _diff_lambda(lam_ref[...], lam_init)
        accn = acc_sc[...] / l_sc[...]
        outs = []
        for h in range(H_DIFF):
            blk = accn[h * rph:(h + 1) * rph]
            oh = blk[0:n_q] - lam * blk[n_q:rph]
            outs.append(_rmsnorm(oh, g_ref[...]) * (1.0 - lam_init))
        o_ref[0] = jnp.concatenate(outs, axis=1)


def _sb_dec_kernel(pt_ref, q_ref, kvnew_ref, *refs, n_pages, n_heads):
    del pt_ref
    pages = refs[:n_pages]
    o_ref = refs[n_pages]
    wq_sc, acc_sc, run_sc, new_sc = refs[n_pages + 1:]
    p = pl.program_id(1)
    n_q, w = q_ref.shape[1], q_ref.shape[2]
    n_rows = n_q * n_heads
    page = pages[0].shape[2]

    def absorb(wq, kts, vts, strict):
        mtri = jnp.where(_strict_upper_sum_matrix(page), 1.0, 0.0).astype(BF16)
        n = len(kts)
        z = jnp.dot(wq, jnp.concatenate(kts, axis=1), preferred_element_type=F32)
        lsz = _log_sigmoid(z)
        lrest = lsz - z
        if strict is not None:
            lrest = jnp.where(strict, lrest, 0.0)
        lrests = [lrest[:, g * page:(g + 1) * page] for g in range(n)]
        cs_all = _suffix_sums_2pass(jnp.concatenate(lrests, axis=0), mtri)
        run = run_sc[...]
        offs = []
        for g in range(n):
            cs = cs_all[g * n_rows:(g + 1) * n_rows]
            offs.append(cs + run)
            run = run + cs[:, 0:1] + lrests[g][:, 0:1]
        a = jnp.exp(lsz + jnp.concatenate(offs, axis=1))
        if strict is not None:
            a = jnp.where(strict, a, 0.0)
        run_sc[...] = run
        acc_sc[...] = acc_sc[...] + lax.dot_general(a.astype(BF16), jnp.concatenate(vts, axis=1), _NT,
                                                    preferred_element_type=F32)

    @pl.when(p == 0)
    def _():
        wq = _build_wq(q_ref[0], n_heads)
        wq_sc[...] = wq
        _fill_new_page(new_sc, kvnew_ref)
        acc_sc[...] = jnp.zeros_like(acc_sc)
        run_sc[...] = jnp.zeros_like(run_sc)
        jj = lax.broadcasted_iota(jnp.int32, (n_rows, page), 1)
        strict = jj < _query_of_row(n_rows, n_heads, page)
        absorb(wq, [new_sc[0].astype(BF16)], [new_sc[1].astype(BF16)], strict)

    absorb(wq_sc[...], [pg[0].astype(BF16) for pg in pages], [pg[1].astype(BF16) for pg in pages], None)

    @pl.when(p == pl.num_programs(1) - 1)
    def _():
        o_ref[0] = _gather_heads(acc_sc[...], n_q, n_heads, _head_mask(n_heads, w, HEAD_DIM))


def _page_specs(pool, layer, n_steps, n_pages):
    total = n_steps * n_pages
    tail = (0,) * (pool.ndim - 2)
    specs = []
    for g in range(n_pages):
        specs.append(pl.BlockSpec(
            (None, None) + pool.shape[2:],
            lambda b, p, pt, g=g: (layer, pt[b, total - 1 - (p * n_pages + g)]) + tail))
    return specs


def _per_batch_spec(shape):
    return pl.BlockSpec((1,) + shape[1:], lambda b, p, pt: (b,) + (0,) * (len(shape) - 1))


def _const_spec(shape):
    return pl.BlockSpec(shape, lambda b, p, pt: (0,) * len(shape))


def _decode_call(kernel_fn, name, page_table, pool, layer, n_pages, batch_args, const_args, extra_pools,
                 n_q, out_w, scratch):
    db, n_tbl = page_table.shape
    n_steps = n_tbl // n_pages
    in_specs = [_per_batch_spec(a.shape) for a in batch_args] + [_const_spec(a.shape) for a in const_args]
    args = list(batch_args) + list(const_args)
    for pl_arr in (pool,) + tuple(extra_pools):
        in_specs += _page_specs(pl_arr, layer, n_steps, n_pages)
        args += [pl_arr] * n_pages
    return pl.pallas_call(
        kernel_fn,
        grid_spec=pltpu.PrefetchScalarGridSpec(
            num_scalar_prefetch=1, grid=(db, n_steps), in_specs=in_specs,
            out_specs=pl.BlockSpec((1, n_q, out_w), lambda b, p, pt: (b, 0, 0)),
            scratch_shapes=scratch),
        out_shape=jax.ShapeDtypeStruct((db, n_q, out_w), F32),
        compiler_params=_cparams("parallel", "arbitrary", vmem=DECODE_VMEM_LIMIT),
        name=name,
    )(page_table, *args)


def _pages_per_step(page_table, pool):
    page_bytes = math.prod(pool.shape[2:]) * pool.dtype.itemsize
    return _pick_tile(page_table.shape[1], max(1, PAGE_STEP_BYTES // page_bytes))


def _transposed_pool(cache):
    l, n_pool, page, two, h, dd = cache.shape
    return cache.transpose(0, 1, 3, 4, 5, 2).reshape(l, n_pool, two, h * dd, page)


def _transposed_new(kv_s, db, n_q, width):
    return kv_s.reshape(db, n_q, 2, width).transpose(0, 2, 3, 1)


def _ab_layer(xp, xs, dims, page_table, cache_kv_a, cache_lf, cache_kv_b, layer, g, w_in, b_f, lam_p,
              subln_g, w_out, table, bias_blk, dec_bias, lam_init, tq):
    b, t, db, n_q = dims
    splits = [0, FOX_W, 2 * FOX_W, 3 * FOX_W, 3 * FOX_W + H_FOX, 3 * FOX_W + H_FOX + DIFF_W,
              3 * FOX_W + H_FOX + 2 * DIFF_W, 3 * FOX_W + H_FOX + 3 * DIFF_W]
    q_a, k_a, v_a, f_a, q_b, k_b, v_b = [w_in[:, splits[i]:splits[i + 1]] for i in range(7)]
    wq = jnp.concatenate([q_a, q_b], axis=1).astype(BF16)
    wkva = jnp.concatenate([k_a, v_a], axis=1).astype(BF16)
    wkvb = jnp.concatenate([k_b, v_b], axis=1).astype(BF16)
    wf = jnp.pad(f_a, ((0, 0), (0, F_PAD - H_FOX))).astype(BF16)
    bf = jnp.pad(b_f, (0, F_PAD - H_FOX)).reshape(1, F_PAD)
    wo_a = w_out[:FOX_W].astype(BF16)
    wo_b = w_out[FOX_W:].astype(BF16)
    wk = jnp.concatenate([k_a, k_b], axis=1).astype(BF16)
    q, k, vt, kvat, kvb, lf = _proj_t(xp, g, b, t, wq, wk, wkva.T,
                                      (v_b.astype(BF16), v_b.T.astype(BF16), wf, bf), H_FOX)
    ft = _cumsum_lanes(lf.reshape(b, t, H_FOX).transpose(0, 2, 1))
    fc = ft.transpose(0, 2, 1)
    o_a = _fox_attn(q, k, vt, ft, fc, b, t, tq, 0)
    o_b = _diff_attn(q, k, vt, table, bias_blk, lam_p, subln_g, lam_init, b, t, tq, FOX_W // ATT_W)
    pre_p = ([o_a, o_b], [wo_a, wo_b])
    kva = kvat.reshape(b, 2, H_FOX, HEAD_DIM, t).transpose(0, 4, 1, 2, 3)

    qkv_s, kva_s, kvb_s, lf_s = _proj(xs, g, wq, [wkva, wkvb], wf, bf, H_FOX)
    q_s = qkv_s[:, :FOX_W + DIFF_W].astype(F32).reshape(db, n_q, FOX_W + DIFF_W)
    lfn = lf_s.reshape(db, n_q, H_FOX).transpose(0, 2, 1)
    n_pool, page = cache_kv_a.shape[1], cache_kv_a.shape[2]
    pool_a = _transposed_pool(cache_kv_a)
    pool_lf = cache_lf.transpose(0, 1, 3, 2)
    pool_b = cache_kv_b.reshape(cache_kv_b.shape[0], n_pool, page * 2 * H_DIFF, 2 * HEAD_DIM)
    n_pages = _pages_per_step(page_table, pool_a)
    rows_a = n_q * H_FOX
    o_a_s = _decode_call(
        functools.partial(_fox_dec_kernel, n_pages=n_pages), "fox_decode", page_table, pool_a, layer, n_pages,
        [q_s[:, :, :FOX_W], _transposed_new(kva_s, db, n_q, FOX_W), lfn], [], [pool_lf], n_q, FOX_W,
        [pltpu.VMEM((rows_a, FOX_W), BF16), pltpu.VMEM((rows_a, 1), F32), pltpu.VMEM((rows_a, 1), F32),
         pltpu.VMEM((rows_a, FOX_W), F32), pltpu.VMEM((H_FOX, 1), F32), pltpu.VMEM((rows_a, 1), F32),
         pltpu.VMEM((2, FOX_W, page), F32)])
    qb = q_s[:, :, FOX_W:].reshape(db, n_q, H_DIFF, 2, HEAD_DIM).transpose(0, 2, 3, 1, 4)
    q_rows = (qb[:, :, :, :, None, :] * jnp.eye(2, dtype=F32)[None, None, :, None, :, None]).reshape(
        db, H_DIFF * 2 * n_q, 2 * HEAD_DIM)
    rows_b = n_q * 2 * H_DIFF
    o_b_s = _decode_call(
        functools.partial(_diff_dec_kernel, n_pages=n_pages, lam_init=lam_init), "diff_decode", page_table,
        pool_b, layer, n_pages,
        [q_rows, kvb_s.reshape(db, n_q * 2 * H_DIFF, 2 * HEAD_DIM)],
        [dec_bias, lam_p, subln_g.reshape(1, LANES)], [], n_q, DIFF_W,
        [pltpu.VMEM((rows_b, 1), F32), pltpu.VMEM((rows_b, 1), F32), pltpu.VMEM((rows_b, 2 * HEAD_DIM), F32),
         pltpu.VMEM((page * 2 * H_DIFF, 2 * HEAD_DIM), F32)])
    pre_s = ([o_a_s.reshape(db * n_q, FOX_W), o_b_s.reshape(db * n_q, DIFF_W)], [wo_a, wo_b])
    return pre_p, pre_s, (kva, lf, kvb), (kva_s, lf_s, kvb_s)


def _sb_layer(xp, xs, dims, page_table, cache_kv, layer, g, w_in, w_out, tq):
    b, t, db, n_q = dims
    width = w_in.shape[1] // 3
    n_heads = width // HEAD_DIM
    wq = w_in[:, :width].astype(BF16)
    wkv = w_in[:, width:].astype(BF16)
    wo = w_out.astype(BF16)

    q, k, vt, kvt = _proj_t(xp, g, b, t, wq, wkv[:, :width], wkv.T)
    o = _sb_attn(q, k, vt, width, b, t, tq)
    pre_p = ([o], [wo])
    kv = kvt.reshape(b, 2, n_heads, HEAD_DIM, t).transpose(0, 4, 1, 2, 3)

    qkv_s, kv_s = _proj(xs, g, wq, [wkv])
    q_s = qkv_s[:, :width].astype(F32).reshape(db, n_q, width)
    page = cache_kv.shape[2]
    pool = _transposed_pool(cache_kv)
    n_pages = _pages_per_step(page_table, pool)
    rows = n_q * n_heads
    o_s = _decode_call(
        functools.partial(_sb_dec_kernel, n_pages=n_pages, n_heads=n_heads), "sb_decode", page_table, pool,
        layer, n_pages, [q_s, _transposed_new(kv_s, db, n_q, width)], [], [], n_q, width,
        [pltpu.VMEM((rows, width), BF16), pltpu.VMEM((rows, width), F32), pltpu.VMEM((rows, 1), F32),
         pltpu.VMEM((2, width, page), F32)])
    pre_s = ([o_s.reshape(db * n_q, width)], [wo])
    return pre_p, pre_s, kv, kv_s


def kernel(x_prompt, x_sample, cache_fox_kv, cache_fox_logf, cache_diff_kv, cache_sb_kv, page_table, rel_bias_table, norm_g, final_norm_g, w_ffn_up, w_ffn_down, w_in_ab, b_forget, diff_lambda, diff_subln_g, w_out_ab, w_in_sb, w_out_sb):
    b, t, d = x_prompt.shape
    db, n_q, _ = x_sample.shape
    depth = norm_g.shape[0]
    page = cache_fox_kv.shape[2]
    assert page >= MAX_DISTANCE and page % LANES == 0 and n_q <= page
    tq = _pick_tile(t, 512)
    assert tq >= MAX_DISTANCE
    dims = (b, t, db, n_q)
    xp = x_prompt.reshape(b * t, d)
    xs = x_sample.reshape(db * n_q, d)
    bias_blk, dec_bias = _bias_tables(rel_bias_table, tq, n_q, page)

    p_fox_kv, p_fox_lf, p_diff_kv, p_sb_kv = [], [], [], []
    s_fox_kv, s_fox_lf, s_diff_kv, s_sb_kv = [], [], [], []
    for i in range(depth):
        l = i // 2
        last = i == depth - 1
        wu = w_ffn_up[i].astype(BF16)
        wd = w_ffn_down[i].astype(BF16)
        xp = _ffn(xp, norm_g[i, 0], wu[0], wd[0])
        xs = _ffn(xs, norm_g[i, 0], wu[0], wd[0])
        if i % 2 == 0:
            lam_init = 0.8 - 0.6 * math.exp(-0.3 * i)
            pre_p, pre_s, (kva, lf, kvb), (kva_s, lf_s, kvb_s) = _ab_layer(
                xp, xs, dims, page_table, cache_fox_kv, cache_fox_logf, cache_diff_kv, l, norm_g[i, 1],
                w_in_ab[l], b_forget[l], diff_lambda[l], diff_subln_g[l], w_out_ab[l], rel_bias_table,
                bias_blk, dec_bias, lam_init, tq)
            p_fox_kv.append(kva.reshape(b, t, 2, H_FOX, HEAD_DIM))
            p_fox_lf.append(lf.reshape(b, t, H_FOX))
            p_diff_kv.append(kvb.reshape(b, t, 2, H_DIFF, 2 * HEAD_DIM))
            s_fox_kv.append(kva_s.reshape(db, n_q, 2, H_FOX, HEAD_DIM))
            s_fox_lf.append(lf_s.reshape(db, n_q, H_FOX))
            s_diff_kv.append(kvb_s.reshape(db, n_q, 2, H_DIFF, 2 * HEAD_DIM))
        else:
            pre_p, pre_s, kv, kv_s = _sb_layer(xp, xs, dims, page_table, cache_sb_kv, l, norm_g[i, 1],
                                         w_in_sb[l], w_out_sb[l], tq)
            n_heads = w_in_sb.shape[-1] // (3 * HEAD_DIM)
            p_sb_kv.append(kv.reshape(b, t, 2, n_heads, HEAD_DIM))
            s_sb_kv.append(kv_s.reshape(db, n_q, 2, n_heads, HEAD_DIM))
        fg = final_norm_g if last else None
        xp = _ffn(xp, norm_g[i, 2], wu[1], wd[1], final_g=fg, pre=pre_p)
        xs = _ffn(xs, norm_g[i, 2], wu[1], wd[1], final_g=fg, pre=pre_s)
    return (xp.reshape(b, t, d), xs.reshape(db, n_q, d),
            jnp.stack(p_fox_kv), jnp.stack(p_fox_lf), jnp.stack(p_diff_kv), jnp.stack(p_sb_kv),
            jnp.stack(s_fox_kv), jnp.stack(s_fox_lf), jnp.stack(s_diff_kv), jnp.stack(s_sb_kv))
```
